```python
import jax, jax.numpy as jnp
from jax import lax
import numpy as np

D_MODEL = 1024
BATCH = 16
SEQ = 4096
DEPTH = 2
DEC_BATCH = 8
DEC_SEQ = 32
PAST_LEN = 1024

CHUNK = 64
D_MIX = D_MODEL
W_GROUP = D_MIX // 4
POOL_WINDOWS = (2, 4, 8, 16)
N_POOL_GROUPS = len(POOL_WINDOWS)
POOL_GC = W_GROUP // N_POOL_GROUPS
POOL_HIST = max(POOL_WINDOWS) - 1
CONF_K = 31
SCONV_K = 3
MLP_CHUNK = 128
MLP_HEADS = 4
MLP_HD = W_GROUP // MLP_HEADS
D_IN = 8 * W_GROUP
D_FF = ((-(-8 * D_MODEL // 3) + 255) // 256) * 256
EPS = 1e-6

kernel_name = "hybrid_streaming_encoder_step"


def rms_norm(x, g):
    xf = x.astype(jnp.float32)
    y = xf * lax.rsqrt(jnp.mean(xf * xf, axis=-1, keepdims=True) + EPS)
    return (y * g.astype(jnp.float32)).astype(x.dtype)


def causal_depthwise(xh, w):
    return lax.conv_general_dilated(
        xh, w[:, None, :].astype(xh.dtype), (1,), 'VALID',
        dimension_numbers=('NWC', 'WIO', 'NWC'), feature_group_count=xh.shape[-1])


def pool_mixer(xh, start_pos, w_pool, scale):
    B, T, C = xh.shape
    L = T - POOL_HIST
    xf = xh.astype(jnp.float32)
    cs = jnp.concatenate([jnp.zeros((B, 1, C), jnp.float32), jnp.cumsum(xf, axis=1)], axis=1)
    pos = start_pos + jnp.arange(L)
    outs = []
    for g, w in enumerate(POOL_WINDOWS):
        sl = slice(g * POOL_GC, (g + 1) * POOL_GC)
        s = cs[:, POOL_HIST + 1:POOL_HIST + 1 + L, sl] - cs[:, POOL_HIST + 1 - w:POOL_HIST + 1 - w + L, sl]
        cnt = jnp.minimum(w, pos + 1).astype(jnp.float32)
        outs.append(s / cnt[None, :, None] - xf[:, POOL_HIST:, sl])
    p = jnp.stack(outs, axis=2)
    y = jnp.einsum('blgc,gcd->blgd', p, w_pool.astype(jnp.float32)).reshape(B, L, C)
    y = y * scale.astype(jnp.float32)
    return y.astype(xh.dtype), xh[:, -POOL_HIST:]


def conformer_conv(a, gate, hist, w_dw, b_dw, ln_g, ln_b):
    z = a * jax.nn.sigmoid(gate)
    zh = jnp.concatenate([hist.astype(z.dtype), z], axis=1)
    c = (causal_depthwise(zh, w_dw) + b_dw).astype(jnp.float32)
    mu = jnp.mean(c, axis=-1, keepdims=True)
    var = jnp.mean(jnp.square(c - mu), axis=-1, keepdims=True)
    n = (c - mu) * lax.rsqrt(var + EPS) * ln_g.astype(jnp.float32) + ln_b.astype(jnp.float32)
    return jax.nn.silu(n).astype(a.dtype), zh[:, -(CONF_K - 1):]


def short_conv(xs, bg, cg, hist, w):
    z = cg * xs
    zh = jnp.concatenate([hist.astype(z.dtype), z], axis=1)
    return bg * causal_depthwise(zh, w), zh[:, -(SCONV_K - 1):]


def chunk_mlp(u, v, w_s, b_s):
    B, L, C = v.shape
    Lc = min(L, MLP_CHUNK)
    n = L // Lc
    ws = w_s[:, :Lc, :Lc] * jnp.tril(jnp.ones((Lc, Lc), w_s.dtype))
    vr = v.reshape(B, n, Lc, MLP_HEADS, MLP_HD)
    mixed = jnp.einsum('hij,bnjhd->bnihd', ws, vr) + b_s[:, :Lc].T[None, None, :, :, None]
    return u * mixed.reshape(B, L, C).astype(u.dtype)


def run_trunk(x, start_pos, pool_h, conv_h, sc_h, g_mix, w_in, w_pool, pool_scale,
              w_conf_dw, b_conf_dw, conf_ln_g, conf_ln_b, w_sconv, w_s, b_s, w_out,
              g_ffn, w_gate, w_up, w_down, g_final):
    B = x.shape[0]
    new_pool, new_conv, new_sc, new_v = [], [], [], []
    for l in range(DEPTH):
        if pool_h is None:
            ph = jnp.zeros((B, POOL_HIST, W_GROUP), x.dtype)
            ch = jnp.zeros((B, CONF_K - 1, W_GROUP), x.dtype)
            sh = jnp.zeros((B, SCONV_K - 1, W_GROUP), x.dtype)
        else:
            ph, ch, sh = pool_h[l].astype(x.dtype), conv_h[l], sc_h[l]
        h = rms_norm(x, g_mix[l])
        p = jnp.einsum('bld,de->ble', h, w_in[l])
        seg = [p[..., i * W_GROUP:(i + 1) * W_GROUP] for i in range(8)]
        ya, ph_new = pool_mixer(jnp.concatenate([ph, seg[0]], axis=1), start_pos, w_pool[l], pool_scale[l])
        yb, ch_new = conformer_conv(seg[1], seg[2], ch, w_conf_dw[l], b_conf_dw[l], conf_ln_g[l], conf_ln_b[l])
        yc, sh_new = short_conv(seg[3], seg[4], seg[5], sh, w_sconv[l])
        yd = chunk_mlp(seg[6], seg[7], w_s[l], b_s[l])
        mix = jnp.concatenate([ya, yb, yc, yd], axis=-1)
        x = x + jnp.einsum('ble,ed->bld', mix, w_out[l])
        f = rms_norm(x, g_ffn[l])
        hid = jax.nn.silu(jnp.einsum('bld,df->blf', f, w_gate[l])) * jnp.einsum('bld,df->blf', f, w_up[l])
        x = x + jnp.einsum('blf,fd->bld', hid, w_down[l])
        new_pool.append(ph_new)
        new_conv.append(ch_new)
        new_sc.append(sh_new)
        new_v.append(seg[7])
    y = rms_norm(x, g_final)
    return y, jnp.stack(new_pool), jnp.stack(new_conv), jnp.stack(new_sc), jnp.stack(new_v)


def setup_inputs(seed: int = 0) -> dict:
    key = jax.random.key(seed)
    ks = jax.random.split(key, 24)
    nrm = lambda k, s, sc: jax.random.normal(k, s, jnp.float32) * sc
    return {
        "x_prompt": nrm(ks[0], (BATCH, SEQ, D_MODEL), 1.0),
        "x_sample": nrm(ks[1], (DEC_BATCH, DEC_SEQ, D_MODEL), 1.0),
        "state_pool": nrm(ks[2], (DEPTH, DEC_BATCH, POOL_HIST, W_GROUP), 1.0),
        "state_conv": nrm(ks[3], (DEPTH, DEC_BATCH, CONF_K - 1, W_GROUP), 0.5),
        "state_short_conv": nrm(ks[4], (DEPTH, DEC_BATCH, SCONV_K - 1, W_GROUP), 0.5),
        "g_mix": 1.0 + nrm(ks[5], (DEPTH, D_MODEL), 0.1),
        "w_in": nrm(ks[6], (DEPTH, D_MODEL, D_IN), D_MODEL ** -0.5),
        "w_pool": nrm(ks[7], (DEPTH, N_POOL_GROUPS, POOL_GC, POOL_GC), POOL_GC ** -0.5),
        "pool_scale": 1.0 + nrm(ks[8], (DEPTH, W_GROUP), 0.1),
        "w_conf_dw": nrm(ks[9], (DEPTH, CONF_K, W_GROUP), CONF_K ** -0.5),
        "b_conf_dw": nrm(ks[10], (DEPTH, W_GROUP), 0.02),
        "conf_ln_g": 1.0 + nrm(ks[11], (DEPTH, W_GROUP), 0.1),
        "conf_ln_b": nrm(ks[12], (DEPTH, W_GROUP), 0.02),
        "w_sconv": nrm(ks[13], (DEPTH, SCONV_K, W_GROUP), SCONV_K ** -0.5),
        "w_s": nrm(ks[14], (DEPTH, MLP_HEADS, MLP_CHUNK, MLP_CHUNK), MLP_CHUNK ** -0.5),
        "b_s": 1.0 + nrm(ks[15], (DEPTH, MLP_HEADS, MLP_CHUNK), 0.1),
        "w_out": nrm(ks[16], (DEPTH, D_MIX, D_MODEL), D_MIX ** -0.5),
        "g_ffn": 1.0 + nrm(ks[17], (DEPTH, D_MODEL), 0.1),
        "w_gate": nrm(ks[18], (DEPTH, D_MODEL, D_FF), D_MODEL ** -0.5),
        "w_up": nrm(ks[19], (DEPTH, D_MODEL, D_FF), D_MODEL ** -0.5),
        "w_down": nrm(ks[20], (DEPTH, D_FF, D_MODEL), D_FF ** -0.5),
        "g_final": 1.0 + nrm(ks[21], (D_MODEL,), 0.1),
    }


def reference(x_prompt, x_sample, state_pool, state_conv, state_short_conv, g_mix, w_in,
              w_pool, pool_scale, w_conf_dw, b_conf_dw, conf_ln_g, conf_ln_b, w_sconv,
              w_s, b_s, w_out, g_ffn, w_gate, w_up, w_down, g_final):
    assert x_sample.shape[1] <= CHUNK
    weights = (g_mix, w_in, w_pool, pool_scale, w_conf_dw, b_conf_dw, conf_ln_g, conf_ln_b,
               w_sconv, w_s, b_s, w_out, g_ffn, w_gate, w_up, w_down, g_final)
    y_prompt, pool_p, conv_p, sc_p, _ = run_trunk(x_prompt, 0, None, None, None, *weights)
    y_sample, pool_s, conv_s, sc_s, v_s = run_trunk(
        x_sample, PAST_LEN, state_pool, state_conv, state_short_conv, *weights)
    return (y_prompt, y_sample, pool_p, pool_s, conv_p, conv_s, sc_p, sc_s, v_s)
```

```python
import functools

import jax
import jax.numpy as jnp
from jax import lax
from jax.experimental import pallas as pl
from jax.experimental.pallas import tpu as pltpu

EPS = 1e-6
W_GROUP = 256
POOL_WINDOWS = (2, 4, 8, 16)
POOL_GC = W_GROUP // len(POOL_WINDOWS)
POOL_HIST = max(POOL_WINDOWS) - 1
CONF_K = 31
SCONV_K = 3
MLP_CHUNK = 128
MLP_HEADS = 4
MLP_HD = W_GROUP // MLP_HEADS
PAST_LEN = 1024

SUBLANES = 8
HIST_PAD = 32
SC_PAD = 8
CONV_ROWS = 64
VMEM_LIMIT_BYTES = 56 * 1024 * 1024

_F32 = jnp.float32
_BF16 = jnp.bfloat16


def _rms_norm(x, g):
    return x * lax.rsqrt(jnp.mean(x * x, axis=-1, keepdims=True) + EPS) * g


def _dot(a, b):
    return jnp.dot(a, b, preferred_element_type=_F32)


def _mixer_kernel(x_ref, hp_ref, hc_ref, hs_ref, gmix_ref, win_ref, wpool_ref, pscale_ref,
                  wdw_ref, bdw_ref, lng_ref, lnb_ref, wsc_ref, wcat_ref, bias_ref, wout_ref,
                  *rest, tile, chunk, start_pos, emit_v):
    if emit_v:
        xo_ref, np_ref, nc_ref, ns_ref, v_ref = rest[:5]
        scratch = rest[5:]
    else:
        xo_ref, np_ref, nc_ref, ns_ref = rest[:4]
        v_ref = None
        scratch = rest[4:]
    pbuf, s2buf, s4buf, s8buf, zbuf, cbuf, mixbuf = scratch
    T, G, H = tile, W_GROUP, HIST_PAD
    l = pl.program_id(1)

    @pl.when(l == 0)
    def _load_history():
        pbuf[0:H, :] = jnp.zeros((H, G), _F32)
        pbuf[H - POOL_HIST:H, :] = hp_ref[0]
        zbuf[0:H, :] = jnp.zeros((H, G), _F32)
        zbuf[H - (CONF_K - 1):H, :] = hc_ref[0]
        cbuf[0:SC_PAD, :] = jnp.zeros((SC_PAD, G), _F32)
        cbuf[SC_PAD - (SCONV_K - 1):SC_PAD, :] = hs_ref[0]

    x = x_ref[0]
    h = _rms_norm(x, gmix_ref[...]).astype(_BF16)

    p_pool = _dot(h, win_ref[:, 0:G])
    pbuf[H:H + T, :] = p_pool
    s2buf[8:H + T, :] = pbuf[8:H + T, :] + pbuf[7:H + T - 1, :]
    s4buf[16:H + T, :] = s2buf[16:H + T, :] + s2buf[14:H + T - 2, :]
    s8buf[24:H + T, :] = s4buf[24:H + T, :] + s4buf[20:H + T - 4, :]
    s16 = s8buf[H:H + T, :] + s8buf[H - 8:H + T - 8, :]
    lane = lax.broadcasted_iota(jnp.int32, (T, G), 1)
    row = lax.broadcasted_iota(jnp.int32, (T, G), 0)
    wsum = jnp.where(lane < POOL_GC, s2buf[H:H + T, :],
                     jnp.where(lane < 2 * POOL_GC, s4buf[H:H + T, :],
                               jnp.where(lane < 3 * POOL_GC, s8buf[H:H + T, :], s16)))
    width = jnp.where(lane < POOL_GC, POOL_WINDOWS[0],
                      jnp.where(lane < 2 * POOL_GC, POOL_WINDOWS[1],
                                jnp.where(lane < 3 * POOL_GC, POOL_WINDOWS[2], POOL_WINDOWS[3])))
    cnt = jnp.minimum(width, start_pos + l * T + row + 1).astype(_F32)
    pooled = wsum / cnt - p_pool
    ya = _dot(pooled.astype(_BF16), wpool_ref[...]) * pscale_ref[...]
    mixbuf[:, 0:G] = ya.astype(_BF16)
    np_ref[0] = pbuf[H + T - POOL_HIST:H + T, :]
    pbuf[0:H, :] = pbuf[T:T + H, :]

    ag = _dot(h, win_ref[:, G:3 * G])
    zbuf[H:H + T, :] = ag[:, 0:G] * jax.nn.sigmoid(ag[:, G:2 * G])
    R = min(CONV_ROWS, T)
    for r0 in range(0, T, R):
        acc = jnp.zeros((R, G), _F32)
        for k in range(CONF_K):
            off = H - (CONF_K - 1) + k + r0
            acc = acc + wdw_ref[k:k + 1, :] * zbuf[off:off + R, :]
        c = acc + bdw_ref[...]
        mu = jnp.mean(c, axis=-1, keepdims=True)
        d = c - mu
        var = jnp.mean(d * d, axis=-1, keepdims=True)
        n = d * lax.rsqrt(var + EPS) * lng_ref[...] + lnb_ref[...]
        mixbuf[r0:r0 + R, G:2 * G] = (n * jax.nn.sigmoid(n)).astype(_BF16)
    nc_ref[0] = zbuf[H + T - (CONF_K - 1):H + T, :]
    zbuf[0:H, :] = zbuf[T:T + H, :]

    sc = _dot(h, win_ref[:, 3 * G:6 * G])
    cbuf[SC_PAD:SC_PAD + T, :] = sc[:, 2 * G:3 * G] * sc[:, 0:G]
    conv = jnp.zeros((T, G), _F32)
    for k in range(SCONV_K):
        off = SC_PAD - (SCONV_K - 1) + k
        conv = conv + wsc_ref[k:k + 1, :] * cbuf[off:off + T, :]
    mixbuf[:, 2 * G:3 * G] = (sc[:, G:2 * G] * conv).astype(_BF16)
    ns_ref[0] = cbuf[SC_PAD + T - (SCONV_K - 1):SC_PAD + T, :]
    cbuf[0:SC_PAD, :] = cbuf[T:T + SC_PAD, :]

    uv = _dot(h, win_ref[:, 6 * G:8 * G])
    if emit_v:
        v_ref[0] = uv[:, G:2 * G]
    Lc = chunk
    wrow = lax.broadcasted_iota(jnp.int32, (Lc, MLP_HEADS * Lc), 0)
    wcol = lax.broadcasted_iota(jnp.int32, (Lc, MLP_HEADS * Lc), 1)
    wcat = jnp.where(lax.rem(wcol, Lc) <= wrow, wcat_ref[...], 0.0).astype(_BF16)
    head = lax.broadcasted_iota(jnp.int32, (Lc, G), 1) // MLP_HD
    for c0 in range(0, T, Lc):
        vb = uv[c0:c0 + Lc, G:2 * G].astype(_BF16)
        rhs = jnp.concatenate(
            [jnp.where(head == hh, vb, jnp.zeros_like(vb)) for hh in range(MLP_HEADS)], axis=0)
        mixed = _dot(wcat, rhs) + bias_ref[...]
        mixbuf[c0:c0 + Lc, 3 * G:4 * G] = (uv[c0:c0 + Lc, 0:G] * mixed).astype(_BF16)

    xo_ref[0] = x + _dot(mixbuf[...], wout_ref[...])


def _ffn_kernel(x_ref, g_ref, wg_ref, wu_ref, wd_ref, gfin_ref, o_ref, *, ff_chunks, final_norm):
    x = x_ref[...]
    f = _rms_norm(x, g_ref[...]).astype(_BF16)
    y = x
    for c0, c1 in ff_chunks:
        gate = _dot(f, wg_ref[:, c0:c1])
        up = _dot(f, wu_ref[:, c0:c1])
        hid = (gate * jax.nn.sigmoid(gate) * up).astype(_BF16)
        y = y + _dot(hid, wd_ref[c0:c1, :])
    if final_norm:
        y = _rms_norm(y, gfin_ref[...])
    o_ref[...] = y


def _resident(shape):
    return pl.BlockSpec(shape, lambda *_: (0,) * len(shape), pipeline_mode=pl.Buffered(1))


def _mixer_call(x, hist_pool, hist_conv, hist_sc, lw, *, tile, start_pos, emit_v):
    B, L, D = x.shape
    G = W_GROUP
    assert L % tile == 0 and tile % SUBLANES == 0 and tile >= HIST_PAD
    chunk = min(L, MLP_CHUNK)
    assert tile % chunk == 0
    n_l = L // tile
    seq_block = lambda rows: pl.BlockSpec((1, rows, G), lambda b, l: (b, 0, 0))
    in_specs = [
        pl.BlockSpec((1, tile, D), lambda b, l: (b, l, 0)),
        seq_block(POOL_HIST), seq_block(CONF_K - 1), seq_block(SCONV_K - 1),
        _resident((1, D)), _resident(lw["w_in"].shape), _resident((G, G)), _resident((1, G)),
        _resident((CONF_K, G)), _resident((1, G)), _resident((1, G)), _resident((1, G)),
        _resident((SCONV_K, G)), _resident((chunk, MLP_HEADS * chunk)), _resident((chunk, G)),
        _resident(lw["w_out"].shape),
    ]
    out_shape = [
        jax.ShapeDtypeStruct((B, L, D), _F32),
        jax.ShapeDtypeStruct((B, POOL_HIST, G), _F32),
        jax.ShapeDtypeStruct((B, CONF_K - 1, G), _F32),
        jax.ShapeDtypeStruct((B, SCONV_K - 1, G), _F32),
    ]
    out_specs = [
        pl.BlockSpec((1, tile, D), lambda b, l: (b, l, 0)),
        seq_block(POOL_HIST), seq_block(CONF_K - 1), seq_block(SCONV_K - 1),
    ]
    if emit_v:
        out_shape.append(jax.ShapeDtypeStruct((B, L, G), _F32))
        out_specs.append(pl.BlockSpec((1, tile, G), lambda b, l: (b, l, 0)))
    hist_rows = HIST_PAD + tile
    scratch_shapes = [pltpu.VMEM((hist_rows, G), _F32) for _ in range(5)] + [
        pltpu.VMEM((SC_PAD + tile, G), _F32), pltpu.VMEM((tile, D), _BF16)]
    wcat = jnp.transpose(lw["w_s"][:, :chunk, :chunk], (1, 0, 2)).reshape(chunk, MLP_HEADS * chunk)
    bias = jnp.repeat(lw["b_s"][:, :chunk].T, MLP_HD, axis=1)
    kernel = functools.partial(_mixer_kernel, tile=tile, chunk=chunk, start_pos=start_pos, emit_v=emit_v)
    return pl.pallas_call(
        kernel, grid=(B, n_l), in_specs=in_specs, out_specs=out_specs, out_shape=out_shape,
        scratch_shapes=scratch_shapes, name="mixer",
        compiler_params=pltpu.CompilerParams(
            dimension_semantics=("arbitrary", "arbitrary"), vmem_limit_bytes=VMEM_LIMIT_BYTES),
    )(x, hist_pool, hist_conv, hist_sc, lw["g_mix"], lw["w_in"], lw["w_pool"], lw["pool_scale"],
      lw["w_conf_dw"], lw["b_conf_dw"], lw["conf_ln_g"], lw["conf_ln_b"], lw["w_sconv"],
      wcat, bias, lw["w_out"])


def _ff_chunks(d_ff, width):
    return tuple((c0, min(c0 + width, d_ff)) for c0 in range(0, d_ff, width))


def _ffn_call(x2d, lw, g_final, *, tile, final_norm):
    N, D = x2d.shape
    d_ff = lw["w_gate"].shape[1]
    assert N % tile == 0
    kernel = functools.partial(_ffn_kernel, ff_chunks=_ff_chunks(d_ff, 1536), final_norm=final_norm)
    return pl.pallas_call(
        kernel, grid=(N // tile,),
        in_specs=[pl.BlockSpec((tile, D), lambda i: (i, 0)), _resident((1, D)),
                  _resident((D, d_ff)), _resident((D, d_ff)), _resident((d_ff, D)), _resident((1, D))],
        out_specs=pl.BlockSpec((tile, D), lambda i: (i, 0)),
        out_shape=jax.ShapeDtypeStruct((N, D), _F32), name="ffn",
        compiler_params=pltpu.CompilerParams(
            dimension_semantics=("arbitrary",), vmem_limit_bytes=VMEM_LIMIT_BYTES),
    )(x2d, lw["g_ffn"], lw["w_gate"], lw["w_up"], lw["w_down"], g_final)


def _block_diag(w):
    g, n, m = w.shape
    out = jnp.zeros((g * n, g * m), w.dtype)
    for i in range(g):
        out = out.at[i * n:(i + 1) * n, i * m:(i + 1) * m].set(w[i])
    return out


def _run_trunk(x, start_pos, hists, layers, g_final, *, mixer_tile, ffn_tile, emit_v):
    B, L, D = x.shape
    new_pool, new_conv, new_sc, new_v = [], [], [], []
    for li, lw in enumerate(layers):
        outs = _mixer_call(x, hists[0][li], hists[1][li], hists[2][li], lw,
                           tile=mixer_tile, start_pos=start_pos, emit_v=emit_v)
        x = outs[0]
        new_pool.append(outs[1])
        new_conv.append(outs[2])
        new_sc.append(outs[3])
        if emit_v:
            new_v.append(outs[4])
        x = _ffn_call(x.reshape(B * L, D), lw, g_final, tile=ffn_tile,
                      final_norm=li == len(layers) - 1).reshape(B, L, D)
    stack = lambda xs: jnp.stack(xs) if xs else None
    return x, stack(new_pool), stack(new_conv), stack(new_sc), stack(new_v)


def kernel(x_prompt, x_sample, state_pool, state_conv, state_short_conv, g_mix, w_in, w_pool, pool_scale, w_conf_dw, b_conf_dw, conf_ln_g, conf_ln_b, w_sconv, w_s, b_s, w_out, g_ffn, w_gate, w_up, w_down, g_final):
    depth = w_in.shape[0]
    row = lambda v: v.reshape(1, -1)
    layers = [dict(
        g_mix=row(g_mix[i]), w_in=w_in[i].astype(_BF16), w_pool=_block_diag(w_pool[i]).astype(_BF16),
        pool_scale=row(pool_scale[i]), w_conf_dw=w_conf_dw[i], b_conf_dw=row(b_conf_dw[i]),
        conf_ln_g=row(conf_ln_g[i]), conf_ln_b=row(conf_ln_b[i]), w_sconv=w_sconv[i],
        w_s=w_s[i], b_s=b_s[i], w_out=w_out[i].astype(_BF16), g_ffn=row(g_ffn[i]),
        w_gate=w_gate[i].astype(_BF16), w_up=w_up[i].astype(_BF16), w_down=w_down[i].astype(_BF16),
    ) for i in range(depth)]
    gfin = row(g_final)

    B, L, _ = x_prompt.shape
    zeros = lambda rows: jnp.zeros((depth, B, rows, W_GROUP), _F32)
    y_p, pool_p, conv_p, sc_p, _ = _run_trunk(
        x_prompt, 0, (zeros(POOL_HIST), zeros(CONF_K - 1), zeros(SCONV_K - 1)), layers, gfin,
        mixer_tile=min(L, 512), ffn_tile=min(B * L, 512), emit_v=False)

    Bs, Ls, _ = x_sample.shape
    y_s, pool_s, conv_s, sc_s, v_s = _run_trunk(
        x_sample, PAST_LEN,(state_pool, state_conv, state_short_conv), layers, gfin,
        mixer_tile=Ls, ffn_tile=Bs * Ls, emit_v=True)
    return (y_p, y_s, pool_p, pool_s, conv_p, conv_s, sc_p, sc_s, v_s)
```

```python
import functools

import jax
import jax.numpy as jnp
from jax import lax
from jax.experimental import pallas as pl
from jax.experimental.pallas import tpu as pltpu

EPS = 1e-6
W_GROUP = 256
POOL_WINDOWS = (2, 4, 8, 16)
POOL_GC = W_GROUP // len(POOL_WINDOWS)
POOL_HIST = max(POOL_WINDOWS) - 1
CONF_K = 31
SCONV_K = 3
MLP_CHUNK = 128
MLP_HEADS = 4
MLP_HD = W_GROUP // MLP_HEADS
PAST_LEN = 1024

SUBLANES = 8
HIST_PAD = 32
SC_PAD = 8
CONV_ROWS = 64
VMEM_LIMIT_BYTES = 56 * 1024 * 1024

_F32 = jnp.float32
_BF16 = jnp.bfloat16


def _rms_norm(x, g):
    return x * lax.rsqrt(jnp.mean(x * x, axis=-1, keepdims=True) + EPS) * g


def _dot(a, b):
    return jnp.dot(a, b, preferred_element_type=_F32)


def _mixer_kernel(x_ref, hp_ref, hc_ref, hs_ref, gmix_ref, win_ref, wpool_ref, pscale_ref,
                  wdw_ref, bdw_ref, lng_ref, lnb_ref, wsc_ref, wcat_ref, bias_ref, wout_ref,
                  *rest, tile, chunk, start_pos, emit_v):
    if emit_v:
        xo_ref, np_ref, nc_ref, ns_ref, v_ref = rest[:5]
        scratch = rest[5:]
    else:
        xo_ref, np_ref, nc_ref, ns_ref = rest[:4]
        v_ref = None
        scratch = rest[4:]
    pbuf, s2buf, s4buf, s8buf, zsh, cbuf, hbuf, pseg, mixbuf = scratch
    T, G, H = tile, W_GROUP, HIST_PAD
    l = pl.program_id(1)
    zbuf = zsh.at[0]

    @pl.when(l == 0)
    def _load_history():
        pbuf[0:H, :] = jnp.zeros((H, G), _F32)
        pbuf[H - POOL_HIST:H, :] = hp_ref[0]
        zbuf[0:H, :] = jnp.zeros((H, G), _F32)
        zbuf[H - (CONF_K - 1):H, :] = hc_ref[0]
        cbuf[0:SC_PAD, :] = jnp.zeros((SC_PAD, G), _F32)
        cbuf[SC_PAD - (SCONV_K - 1):SC_PAD, :] = hs_ref[0]

    hbuf[...] = _rms_norm(x_ref[0], gmix_ref[...]).astype(_BF16)
    proj = lambda c0, c1: _dot(hbuf[...], win_ref[:, c0 * G:c1 * G])
    seg = lambda i, r0=0, rows=T: pseg[r0:r0 + rows, (i - 1) * G:i * G]

    def short_conv_in():
        pseg[:, 2 * G:3 * G] = proj(3, 4)
        cbuf[SC_PAD:SC_PAD + T, :] = proj(5, 6) * seg(3)

    def short_conv():
        conv = jnp.zeros((T, G), _F32)
        for k in range(SCONV_K):
            off = SC_PAD - (SCONV_K - 1) + k
            conv = conv + wsc_ref[k:k + 1, :] * cbuf[off:off + T, :]
        mixbuf[:, G:2 * G] = (proj(4, 5) * conv).astype(_BF16)
        ns_ref[0] = cbuf[SC_PAD + T - (SCONV_K - 1):SC_PAD + T, :]
        cbuf[0:SC_PAD, :] = cbuf[T:T + SC_PAD, :]

    def pool_in():
        pbuf[H:H + T, :] = proj(0, 1)

    def pool():
        s2buf[8:H + T, :] = pbuf[8:H + T, :] + pbuf[7:H + T - 1, :]
        s4buf[16:H + T, :] = s2buf[16:H + T, :] + s2buf[14:H + T - 2, :]
        s8buf[24:H + T, :] = s4buf[24:H + T, :] + s4buf[20:H + T - 4, :]
        s16 = s8buf[H:H + T, :] + s8buf[H - 8:H + T - 8, :]
        lane = lax.broadcasted_iota(jnp.int32, (T, G), 1)
        row = lax.broadcasted_iota(jnp.int32, (T, G), 0)
        wsum = jnp.where(lane < POOL_GC, s2buf[H:H + T, :],
                         jnp.where(lane < 2 * POOL_GC, s4buf[H:H + T, :],
                                   jnp.where(lane < 3 * POOL_GC, s8buf[H:H + T, :], s16)))
        width = jnp.where(lane < POOL_GC, POOL_WINDOWS[0],
                          jnp.where(lane < 2 * POOL_GC, POOL_WINDOWS[1],
                                    jnp.where(lane < 3 * POOL_GC, POOL_WINDOWS[2], POOL_WINDOWS[3])))
        cnt = jnp.minimum(width, start_pos + l * T + row + 1).astype(_F32)
        pooled = wsum / cnt - pbuf[H:H + T, :]
        ya = _dot(pooled.astype(_BF16), wpool_ref[...]) * pscale_ref[...]
        mixbuf[:, 0:G] = ya.astype(_BF16)
        np_ref[0] = pbuf[H + T - POOL_HIST:H + T, :]
        pbuf[0:H, :] = pbuf[T:T + H, :]

    def chunk_mlp_in():
        pseg[:, 5 * G:7 * G] = proj(6, 8)
        if emit_v:
            v_ref[0] = seg(7)

    def chunk_mlp():
        Lc = chunk
        wrow = lax.broadcasted_iota(jnp.int32, (Lc, MLP_HEADS * Lc), 0)
        wcol = lax.broadcasted_iota(jnp.int32, (Lc, MLP_HEADS * Lc), 1)
        wcat = jnp.where(lax.rem(wcol, Lc) <= wrow, wcat_ref[...], 0.0).astype(_BF16)
        head = lax.broadcasted_iota(jnp.int32, (Lc, G), 1) // MLP_HD
        for c0 in range(0, T, Lc):
            vb = seg(7, c0, Lc).astype(_BF16)
            rhs = jnp.concatenate(
                [jnp.where(head == hh, vb, jnp.zeros_like(vb)) for hh in range(MLP_HEADS)], axis=0)
            mixed = _dot(wcat, rhs) + bias_ref[...]
            mixbuf[c0:c0 + Lc, 2 * G:3 * G] = (seg(6, c0, Lc) * mixed).astype(_BF16)

    def out_proj_first():
        xo_ref[0] = x_ref[0] + _dot(mixbuf[:, 0:3 * G], wout_ref[0:3 * G, :])

    pseg[:, 0:2 * G] = proj(1, 3)
    zbuf[H:H + T, :] = seg(1) * jax.nn.sigmoid(seg(2))
    for r in range(1, SUBLANES):
        zsh[r, 0:H + T - SUBLANES, :] = zbuf[r:r + H + T - SUBLANES, :]
    others = [short_conv_in, short_conv, pool_in, pool, chunk_mlp_in, chunk_mlp, out_proj_first]
    R = min(CONV_ROWS, T)
    for r0 in range(0, T, R):
        acc = None
        for k in range(CONF_K):
            off = H - (CONF_K - 1) + k
            r = off % SUBLANES
            taps = zsh[r, off - r + r0:off - r + r0 + R, :].reshape(R // SUBLANES, SUBLANES, G)
            term = wdw_ref[k][None] * taps
            acc = term if acc is None else acc + term
        c = acc.reshape(R, G) + bdw_ref[...]
        mu = jnp.mean(c, axis=-1, keepdims=True)
        d = c - mu
        var = jnp.mean(d * d, axis=-1, keepdims=True)
        n = d * lax.rsqrt(var + EPS) * lng_ref[...] + lnb_ref[...]
        mixbuf[r0:r0 + R, 3 * G:4 * G] = (n * jax.nn.sigmoid(n)).astype(_BF16)
        if others:
            others.pop(0)()
    for task in others:
        task()
    nc_ref[0] = zbuf[H + T - (CONF_K - 1):H + T, :]
    zbuf[0:H, :] = zbuf[T:T + H, :]
    xo_ref[0] += _dot(mixbuf[:, 3 * G:4 * G], wout_ref[3 * G:4 * G, :])


def _ffn_kernel(x_ref, g_ref, wg_ref, wu_ref, wd_ref, gfin_ref, o_ref, *, ff_chunks, final_norm):
    x = x_ref[...]
    f = _rms_norm(x, g_ref[...]).astype(_BF16)
    y = x
    for c0, c1 in ff_chunks:
        gate = _dot(f, wg_ref[:, c0:c1])
        up = _dot(f, wu_ref[:, c0:c1])
        hid = (gate * jax.nn.sigmoid(gate) * up).astype(_BF16)
        y = y + _dot(hid, wd_ref[c0:c1, :])
    if final_norm:
        y = _rms_norm(y, gfin_ref[...])
    o_ref[...] = y


def _resident(shape):
    return pl.BlockSpec(shape, lambda *_: (0,) * len(shape), pipeline_mode=pl.Buffered(1))


def _mixer_call(x, hist_pool, hist_conv, hist_sc, lw, *, tile, start_pos, emit_v):
    B, L, D = x.shape
    G = W_GROUP
    assert L % tile == 0 and tile % SUBLANES == 0 and tile >= HIST_PAD
    chunk = min(L, MLP_CHUNK)
    assert tile % chunk == 0
    n_l = L // tile
    seq_block = lambda rows: pl.BlockSpec((1, rows, G), lambda b, l: (b, 0, 0))
    in_specs = [
        pl.BlockSpec((1, tile, D), lambda b, l: (b, l, 0)),
        seq_block(POOL_HIST), seq_block(CONF_K - 1), seq_block(SCONV_K - 1),
        _resident((1, D)), _resident(lw["w_in"].shape), _resident((G, G)), _resident((1, G)),
        _resident((CONF_K, SUBLANES, G)), _resident((1, G)), _resident((1, G)), _resident((1, G)),
        _resident((SCONV_K, G)), _resident((chunk, MLP_HEADS * chunk)), _resident((chunk, G)),
        _resident(lw["w_out"].shape),
    ]
    out_shape = [
        jax.ShapeDtypeStruct((B, L, D), _F32),
        jax.ShapeDtypeStruct((B, POOL_HIST, G), _F32),
        jax.ShapeDtypeStruct((B, CONF_K - 1, G), _F32),
        jax.ShapeDtypeStruct((B, SCONV_K - 1, G), _F32),
    ]
    out_specs = [
        pl.BlockSpec((1, tile, D), lambda b, l: (b, l, 0)),
        seq_block(POOL_HIST), seq_block(CONF_K - 1), seq_block(SCONV_K - 1),
    ]
    if emit_v:
        out_shape.append(jax.ShapeDtypeStruct((B, L, G), _F32))
        out_specs.append(pl.BlockSpec((1, tile, G), lambda b, l: (b, l, 0)))
    hist_rows = HIST_PAD + tile
    scratch_shapes = [pltpu.VMEM((hist_rows, G), _F32) for _ in range(4)] + [
        pltpu.VMEM((SUBLANES, hist_rows, G), _F32), pltpu.VMEM((SC_PAD + tile, G), _F32),
        pltpu.VMEM((tile, D), _BF16), pltpu.VMEM((tile, 7 * G), _F32), pltpu.VMEM((tile, 4 * G), _BF16)]
    wcat = jnp.transpose(lw["w_s"][:, :chunk, :chunk], (1, 0, 2)).reshape(chunk, MLP_HEADS * chunk)
    bias = jnp.repeat(lw["b_s"][:, :chunk].T, MLP_HD, axis=1)
    kernel = functools.partial(_mixer_kernel, tile=tile, chunk=chunk, start_pos=start_pos, emit_v=emit_v)
    return pl.pallas_call(
        kernel, grid=(B, n_l), in_specs=in_specs, out_specs=out_specs, out_shape=out_shape,
        scratch_shapes=scratch_shapes, name="mixer",
        compiler_params=pltpu.CompilerParams(
            dimension_semantics=("arbitrary", "arbitrary"), vmem_limit_bytes=VMEM_LIMIT_BYTES),
    )(x, hist_pool, hist_conv, hist_sc, lw["g_mix"], lw["w_in"], lw["w_pool"], lw["pool_scale"],
      lw["w_conf_dw"], lw["b_conf_dw"], lw["conf_ln_g"], lw["conf_ln_b"], lw["w_sconv"],
      wcat, bias, lw["w_out"])


def _ff_chunks(d_ff, width):
    return tuple((c0, min(c0 + width, d_ff)) for c0 in range(0, d_ff, width))


def _ffn_call(x2d, lw, g_final, *, tile, final_norm):
    N, D = x2d.shape
    d_ff = lw["w_gate"].shape[1]
    assert N % tile == 0
    kernel = functools.partial(_ffn_kernel, ff_chunks=_ff_chunks(d_ff, 1536), final_norm=final_norm)
    return pl.pallas_call(
        kernel, grid=(N // tile,),
        in_specs=[pl.BlockSpec((tile, D), lambda i: (i, 0)), _resident((1, D)),
                  _resident((D, d_ff)), _resident((D, d_ff)), _resident((d_ff, D)), _resident((1, D))],
        out_specs=pl.BlockSpec((tile, D), lambda i: (i, 0)),
        out_shape=jax.ShapeDtypeStruct((N, D), _F32), name="ffn",
        compiler_params=pltpu.CompilerParams(
            dimension_semantics=("arbitrary",), vmem_limit_bytes=VMEM_LIMIT_BYTES),
    )(x2d, lw["g_ffn"], lw["w_gate"], lw["w_up"], lw["w_down"], g_final)


def _block_diag(w):
    g, n, m = w.shape
    out = jnp.zeros((g * n, g * m), w.dtype)
    for i in range(g):
        out = out.at[i * n:(i + 1) * n, i * m:(i + 1) * m].set(w[i])
    return out


def _sublane_rows(w):
    return jnp.broadcast_to(w[:, None, :], (w.shape[0], SUBLANES, w.shape[1]))


def _mixer_major_rows(w_out):
    G = W_GROUP
    return jnp.concatenate([w_out[0:G], w_out[2 * G:4 * G], w_out[G:2 * G]], axis=0)


def _run_trunk(x, start_pos, hists, layers, g_final, *, mixer_tile, ffn_tile, emit_v):
    B, L, D = x.shape
    new_pool, new_conv, new_sc, new_v = [], [], [], []
    for li, lw in enumerate(layers):
        outs = _mixer_call(x, hists[0][li], hists[1][li], hists[2][li], lw,
                           tile=mixer_tile, start_pos=start_pos, emit_v=emit_v)
        x = outs[0]
        new_pool.append(outs[1])
        new_conv.append(outs[2])
        new_sc.append(outs[3])
        if emit_v:
            new_v.append(outs[4])
        x = _ffn_call(x.reshape(B * L, D), lw, g_final, tile=ffn_tile,
                      final_norm=li == len(layers) - 1).reshape(B, L, D)
    stack = lambda xs: jnp.stack(xs) if xs else None
    return x, stack(new_pool), stack(new_conv), stack(new_sc), stack(new_v)


def kernel(x_prompt, x_sample, state_pool, state_conv, state_short_conv, g_mix, w_in, w_pool, pool_scale, w_conf_dw, b_conf_dw, conf_ln_g, conf_ln_b, w_sconv, w_s, b_s, w_out, g_ffn, w_gate, w_up, w_down, g_final):
    depth = w_in.shape[0]
    row = lambda v: v.reshape(1, -1)
    layers = [dict(
        g_mix=row(g_mix[i]), w_in=w_in[i].astype(_BF16), w_pool=_block_diag(w_pool[i]).astype(_BF16),
        pool_scale=row(pool_scale[i]), w_conf_dw=_sublane_rows(w_conf_dw[i]), b_conf_dw=row(b_conf_dw[i]),
        conf_ln_g=row(conf_ln_g[i]), conf_ln_b=row(conf_ln_b[i]), w_sconv=w_sconv[i],
        w_s=w_s[i], b_s=b_s[i], w_out=_mixer_major_rows(w_out[i]).astype(_BF16), g_ffn=row(g_ffn[i]),
        w_gate=w_gate[i].astype(_BF16), w_up=w_up[i].astype(_BF16), w_down=w_down[i].astype(_BF16),
    ) for i in range(depth)]
    gfin = row(g_final)

    B, L, _ = x_prompt.shape
    zeros = lambda rows: jnp.zeros((depth, B, rows, W_GROUP), _F32)
    y_p, pool_p, conv_p, sc_p, _ = _run_trunk(
        x_prompt, 0, (zeros(POOL_HIST), zeros(CONF_K - 1), zeros(SCONV_K - 1)), layers, gfin,
        mixer_tile=min(L, 512), ffn_tile=min(B * L, 512), emit_v=False)

    Bs, Ls, _ = x_sample.shape
    y_s, pool_s, conv_s, sc_s, v_s = _run_trunk(
        x_sample, PAST_LEN, (state_pool, state_conv, state_short_conv), layers, gfin,
        mixer_tile=Ls, ffn_tile=Bs * Ls, emit_v=True)
    return (y_p, y_s, pool_p, pool_s, conv_p, conv_s, sc_p, sc_s, v_s)
```

```python
import functools

import jax
import jax.numpy as jnp
from jax import lax
from jax.experimental import pallas as pl
from jax.experimental.pallas import tpu as pltpu

EPS = 1e-6
W_GROUP = 256
POOL_WINDOWS = (2, 4, 8, 16)
POOL_GC = W_GROUP // len(POOL_WINDOWS)
POOL_HIST = max(POOL_WINDOWS) - 1
CONF_K = 31
SCONV_K = 3
MLP_CHUNK = 128
MLP_HEADS = 4
MLP_HD = W_GROUP // MLP_HEADS
PAST_LEN = 1024

SUBLANES = 8
BF16_ROWS = 16
LANES = 128
HIST_PAD = 32
SC_PAD = 8
CONV_ROWS = 64
VMEM_LIMIT_BYTES = 56 * 1024 * 1024

_F32 = jnp.float32
_BF16 = jnp.bfloat16


def _rms_norm(x, g):
    return x * lax.rsqrt(jnp.mean(x * x, axis=-1, keepdims=True) + EPS) * g


def _dot(a, b):
    return jnp.dot(a, b, preferred_element_type=_F32)


def _mixer_kernel(zero_ref, x_ref, hp_ref, hc_ref, hs_ref, gmix_ref, win_ref, wpool_ref, pscale_ref,
                  wdw_ref, bdw_ref, lng_ref, lnb_ref, wsc_ref, wcat_ref, bias_ref, wout_ref,
                  *rest, tile, chunk, start_pos, emit_v):
    if emit_v:
        xo_ref, np_ref, nc_ref, ns_ref, v_ref = rest[:5]
        scratch = rest[5:]
    else:
        xo_ref, np_ref, nc_ref, ns_ref = rest[:4]
        v_ref = None
        scratch = rest[4:]
    pbuf, s2buf, s4buf, s8buf, zsh, cbuf, hbuf, pseg, mixbuf = scratch
    T, G, H = tile, W_GROUP, HIST_PAD
    l = pl.program_id(1)
    zbuf = zsh.at[0]

    @pl.when(l == 0)
    def _load_history():
        pbuf[0:H, :] = jnp.zeros((H, G), _F32)
        pbuf[H - POOL_HIST:H, :] = hp_ref[0]
        zbuf[0:H, :] = jnp.zeros((H, G), _F32)
        zbuf[H - (CONF_K - 1):H, :] = hc_ref[0]
        cbuf[0:SC_PAD, :] = jnp.zeros((SC_PAD, G), _F32)
        cbuf[SC_PAD - (SCONV_K - 1):SC_PAD, :] = hs_ref[0]

    hbuf[0:T, :] = _rms_norm(x_ref[0], gmix_ref[...]).astype(_BF16)
    proj = lambda c0, c1: _dot(hbuf[0:T, :], win_ref[:, c0 * G:c1 * G])
    seg = lambda i, r0=0, rows=T: pseg[r0:r0 + rows, (i - 1) * G:i * G]
    pad_row = pl.multiple_of(T + zero_ref[0], BF16_ROWS)

    def order_later_projections_after(y):
        hbuf[pl.ds(pad_row, BF16_ROWS), 0:LANES] = y[0:BF16_ROWS, 0:LANES]

    R = min(CONV_ROWS, T)
    conv_rows = list(range(0, T, R))

    def conformer_conv(n_rows):
        y = None
        for _ in range(max(1, n_rows // R)):
            if not conv_rows:
                break
            r0 = conv_rows.pop(0)
            acc = None
            for k in range(CONF_K):
                off = H - (CONF_K - 1) + k
                r = off % SUBLANES
                taps = zsh[r, off - r + r0:off - r + r0 + R, :].reshape(R // SUBLANES, SUBLANES, G)
                term = wdw_ref[k][None] * taps
                acc = term if acc is None else acc + term
            c = acc.reshape(R, G) + bdw_ref[...]
            mu = jnp.mean(c, axis=-1, keepdims=True)
            d = c - mu
            var = jnp.mean(d * d, axis=-1, keepdims=True)
            n = d * lax.rsqrt(var + EPS) * lng_ref[...] + lnb_ref[...]
            y = (n * jax.nn.sigmoid(n)).astype(_BF16)
            mixbuf[r0:r0 + R, 3 * G:4 * G] = y
        return y

    def short_conv():
        conv = None
        for k in range(SCONV_K):
            off = SC_PAD - (SCONV_K - 1) + k
            term = wsc_ref[k:k + 1, :] * cbuf[off:off + T, :]
            conv = term if conv is None else conv + term
        y = (seg(4) * conv).astype(_BF16)
        mixbuf[:, G:2 * G] = y
        ns_ref[0] = cbuf[SC_PAD + T - (SCONV_K - 1):SC_PAD + T, :]
        cbuf[0:SC_PAD, :] = cbuf[T:T + SC_PAD, :]
        return y

    def pool():
        s2buf[8:H + T, :] = pbuf[8:H + T, :] + pbuf[7:H + T - 1, :]
        s4buf[16:H + T, :] = s2buf[16:H + T, :] + s2buf[14:H + T - 2, :]
        s8buf[24:H + T, :] = s4buf[24:H + T, :] + s4buf[20:H + T - 4, :]
        s16 = s8buf[H:H + T, :] + s8buf[H - 8:H + T - 8, :]
        lane = lax.broadcasted_iota(jnp.int32, (T, G), 1)
        row = lax.broadcasted_iota(jnp.int32, (T, G), 0)
        wsum = jnp.where(lane < POOL_GC, s2buf[H:H + T, :],
                         jnp.where(lane < 2 * POOL_GC, s4buf[H:H + T, :],
                                   jnp.where(lane < 3 * POOL_GC, s8buf[H:H + T, :], s16)))
        width = jnp.where(lane < POOL_GC, POOL_WINDOWS[0],
                          jnp.where(lane < 2 * POOL_GC, POOL_WINDOWS[1],
                                    jnp.where(lane < 3 * POOL_GC, POOL_WINDOWS[2], POOL_WINDOWS[3])))
        cnt = jnp.minimum(width, start_pos + l * T + row + 1).astype(_F32)
        pooled = wsum / cnt - pbuf[H:H + T, :]
        ya = _dot(pooled.astype(_BF16), wpool_ref[...]) * pscale_ref[...]
        mixbuf[:, 0:G] = ya.astype(_BF16)
        np_ref[0] = pbuf[H + T - POOL_HIST:H + T, :]
        pbuf[0:H, :] = pbuf[T:T + H, :]

    def chunk_mlp():
        Lc = chunk
        wrow = lax.broadcasted_iota(jnp.int32, (Lc, MLP_HEADS * Lc), 0)
        wcol = lax.broadcasted_iota(jnp.int32, (Lc, MLP_HEADS * Lc), 1)
        wcat = jnp.where(lax.rem(wcol, Lc) <= wrow, wcat_ref[...], 0.0).astype(_BF16)
        head = lax.broadcasted_iota(jnp.int32, (Lc, G), 1) // MLP_HD
        for c0 in range(0, T, Lc):
            vb = seg(7, c0, Lc).astype(_BF16)
            rhs = jnp.concatenate(
                [jnp.where(head == hh, vb, jnp.zeros_like(vb)) for hh in range(MLP_HEADS)], axis=0)
            mixed = _dot(wcat, rhs) + bias_ref[...]
            mixbuf[c0:c0 + Lc, 2 * G:3 * G] = (seg(6, c0, Lc) * mixed).astype(_BF16)

    pseg[:, 0:2 * G] = proj(1, 3)
    zbuf[H:H + T, :] = seg(1) * jax.nn.sigmoid(seg(2))
    for r in range(1, SUBLANES):
        zsh[r, 0:H + T - SUBLANES, :] = zbuf[r:r + H + T - SUBLANES, :]
    pseg[:, 2 * G:3 * G] = proj(3, 4)
    cbuf[SC_PAD:SC_PAD + T, :] = proj(5, 6) * seg(3)
    y = conformer_conv(T // 4)
    if y is not None:
        order_later_projections_after(y)
    pseg[:, 3 * G:4 * G] = proj(4, 5)
    y = conformer_conv(T // 4)
    if y is not None:
        order_later_projections_after(y)
    pbuf[H:H + T, :] = proj(0, 1)
    order_later_projections_after(short_conv())
    pseg[:, 5 * G:7 * G] = proj(6, 8)
    if emit_v:
        v_ref[0] = seg(7)
    conformer_conv(T // 8)
    pool()
    conformer_conv(T // 8)
    chunk_mlp()
    xo_ref[0] = x_ref[0] + _dot(mixbuf[:, 0:3 * G], wout_ref[0:3 * G, :])
    conformer_conv(T)
    nc_ref[0] = zbuf[H + T - (CONF_K - 1):H + T, :]
    zbuf[0:H, :] = zbuf[T:T + H, :]
    xo_ref[0] += _dot(mixbuf[:, 3 * G:4 * G], wout_ref[3 * G:4 * G, :])


def _ffn_kernel(x_ref, g_ref, wg_ref, wu_ref, wd_ref, gfin_ref, o_ref, *, ff_chunks, final_norm):
    x = x_ref[...]
    f = _rms_norm(x, g_ref[...]).astype(_BF16)
    y = x
    for c0, c1 in ff_chunks:
        gate = _dot(f, wg_ref[:, c0:c1])
        up = _dot(f, wu_ref[:, c0:c1])
        hid = (gate * jax.nn.sigmoid(gate) * up).astype(_BF16)
        y = y + _dot(hid, wd_ref[c0:c1, :])
    if final_norm:
        y = _rms_norm(y, gfin_ref[...])
    o_ref[...] = y


def _resident(shape):
    return pl.BlockSpec(shape, lambda *_: (0,) * len(shape), pipeline_mode=pl.Buffered(1))


def _resident_layer(stacked, layer):
    return pl.BlockSpec((None,) + stacked.shape[1:], lambda *_: (layer, 0, 0), pipeline_mode=pl.Buffered(1))


def _mixer_call(x, hist_pool, hist_conv, hist_sc, lw, *, tile, start_pos, emit_v):
    B, L, D = x.shape
    G = W_GROUP
    assert L % tile == 0 and tile % BF16_ROWS == 0 and tile >= HIST_PAD
    chunk = min(L, MLP_CHUNK)
    assert tile % chunk == 0
    n_l = L // tile
    seq_block = lambda rows: pl.BlockSpec((1, rows, G), lambda b, l: (b, 0, 0))
    in_specs = [
        pl.BlockSpec(memory_space=pltpu.SMEM),
        pl.BlockSpec((1, tile, D), lambda b, l: (b, l, 0)),
        seq_block(POOL_HIST), seq_block(CONF_K - 1), seq_block(SCONV_K - 1),
        _resident((1, D)), _resident_layer(lw["w_in"], lw["layer"]), _resident((G, G)), _resident((1, G)),
        _resident((CONF_K, SUBLANES, G)), _resident((1, G)), _resident((1, G)), _resident((1, G)),
        _resident((SCONV_K, G)), _resident((chunk, MLP_HEADS * chunk)), _resident((chunk, G)),
        _resident_layer(lw["w_out"], lw["layer"]),
    ]
    out_shape = [
        jax.ShapeDtypeStruct((B, L, D), _F32),
        jax.ShapeDtypeStruct((B, POOL_HIST, G), _F32),
        jax.ShapeDtypeStruct((B, CONF_K - 1, G), _F32),
        jax.ShapeDtypeStruct((B, SCONV_K - 1, G), _F32),
    ]
    out_specs = [
        pl.BlockSpec((1, tile, D), lambda b, l: (b, l, 0)),
        seq_block(POOL_HIST), seq_block(CONF_K - 1), seq_block(SCONV_K - 1),
    ]
    if emit_v:
        out_shape.append(jax.ShapeDtypeStruct((B, L, G), _F32))
        out_specs.append(pl.BlockSpec((1, tile, G), lambda b, l: (b, l, 0)))
    hist_rows = HIST_PAD + tile
    scratch_shapes = [pltpu.VMEM((hist_rows, G), _F32) for _ in range(4)] + [
        pltpu.VMEM((SUBLANES, hist_rows, G), _F32), pltpu.VMEM((SC_PAD + tile, G), _F32),
        pltpu.VMEM((tile + BF16_ROWS, D), _BF16), pltpu.VMEM((tile, 7 * G), _F32), pltpu.VMEM((tile, 4 * G), _BF16)]
    wcat = jnp.transpose(lw["w_s"][:, :chunk, :chunk], (1, 0, 2)).reshape(chunk, MLP_HEADS * chunk)
    bias = jnp.repeat(lw["b_s"][:, :chunk].T, MLP_HD, axis=1)
    kernel = functools.partial(_mixer_kernel, tile=tile, chunk=chunk, start_pos=start_pos, emit_v=emit_v)
    return pl.pallas_call(
        kernel, grid=(B, n_l), in_specs=in_specs, out_specs=out_specs, out_shape=out_shape,
        scratch_shapes=scratch_shapes, name="mixer",
        compiler_params=pltpu.CompilerParams(
            dimension_semantics=("arbitrary", "arbitrary"), vmem_limit_bytes=VMEM_LIMIT_BYTES),
    )(jnp.zeros((1,), jnp.int32), x, hist_pool, hist_conv, hist_sc, lw["g_mix"], lw["w_in"], lw["w_pool"],
      lw["pool_scale"], lw["w_conf_dw"], lw["b_conf_dw"], lw["conf_ln_g"], lw["conf_ln_b"], lw["w_sconv"],
      wcat, bias, lw["w_out"])


def _ff_chunks(d_ff, width):
    return tuple((c0, min(c0 + width, d_ff)) for c0 in range(0, d_ff, width))


def _ffn_call(x2d, lw, g_final, *, tile, final_norm):
    N, D = x2d.shape
    d_ff = lw["w_gate"].shape[-1]
    assert N % tile == 0
    kernel = functools.partial(_ffn_kernel, ff_chunks=_ff_chunks(d_ff, 1536), final_norm=final_norm)
    return pl.pallas_call(
        kernel, grid=(N // tile,),
        in_specs=[pl.BlockSpec((tile, D), lambda i: (i, 0)), _resident((1, D)),
                  _resident_layer(lw["w_gate"], lw["layer"]), _resident_layer(lw["w_up"], lw["layer"]),
                  _resident_layer(lw["w_down"], lw["layer"]), _resident((1, D))],
        out_specs=pl.BlockSpec((tile, D), lambda i: (i, 0)),
        out_shape=jax.ShapeDtypeStruct((N, D), _F32), name="ffn",
        compiler_params=pltpu.CompilerParams(
            dimension_semantics=("arbitrary",), vmem_limit_bytes=VMEM_LIMIT_BYTES),
    )(x2d, lw["g_ffn"], lw["w_gate"], lw["w_up"], lw["w_down"], g_final)


def _block_diag(w):
    g, n, m = w.shape
    out = jnp.zeros((g * n, g * m), w.dtype)
    for i in range(g):
        out = out.at[i * n:(i + 1) * n, i * m:(i + 1) * m].set(w[i])
    return out


def _sublane_rows(w):
    return jnp.broadcast_to(w[:, None, :], (w.shape[0], SUBLANES, w.shape[1]))


def _mixer_major_rows(w_out):
    G = W_GROUP
    return jnp.concatenate([w_out[:, 0:G], w_out[:, 2 * G:4 * G], w_out[:, G:2 * G]], axis=1)


def _run_trunk(x, start_pos, hists, layers, g_final, *, mixer_tile, ffn_tile, emit_v):
    B, L, D = x.shape
    new_pool, new_conv, new_sc, new_v = [], [], [], []
    for li, lw in enumerate(layers):
        outs = _mixer_call(x, hists[0][li], hists[1][li], hists[2][li], lw,
                           tile=mixer_tile, start_pos=start_pos, emit_v=emit_v)
        x = outs[0]
        new_pool.append(outs[1])
        new_conv.append(outs[2])
        new_sc.append(outs[3])
        if emit_v:
            new_v.append(outs[4])
        x = _ffn_call(x.reshape(B * L, D), lw, g_final, tile=ffn_tile,
                      final_norm=li == len(layers) - 1).reshape(B, L, D)
    stack = lambda xs: jnp.stack(xs) if xs else None
    return x, stack(new_pool), stack(new_conv), stack(new_sc), stack(new_v)


def kernel(x_prompt, x_sample, state_pool, state_conv, state_short_conv, g_mix, w_in, w_pool, pool_scale, w_conf_dw, b_conf_dw, conf_ln_g, conf_ln_b, w_sconv, w_s, b_s, w_out, g_ffn, w_gate, w_up, w_down, g_final):
    depth = w_in.shape[0]
    row = lambda v: v.reshape(1, -1)
    w_in, w_out = w_in.astype(_BF16), _mixer_major_rows(w_out).astype(_BF16)
    w_gate, w_up, w_down = w_gate.astype(_BF16), w_up.astype(_BF16), w_down.astype(_BF16)
    layers = [dict(
        layer=i, w_in=w_in, w_out=w_out, w_gate=w_gate, w_up=w_up, w_down=w_down,
        g_mix=row(g_mix[i]), w_pool=_block_diag(w_pool[i]).astype(_BF16),
        pool_scale=row(pool_scale[i]), w_conf_dw=_sublane_rows(w_conf_dw[i]), b_conf_dw=row(b_conf_dw[i]),
        conf_ln_g=row(conf_ln_g[i]), conf_ln_b=row(conf_ln_b[i]), w_sconv=w_sconv[i],
        w_s=w_s[i], b_s=b_s[i], g_ffn=row(g_ffn[i]),
    ) for i in range(depth)]
    gfin = row(g_final)

    B, L, _ = x_prompt.shape
    zeros = lambda rows: jnp.zeros((depth, B, rows, W_GROUP), _F32)
    y_p, pool_p, conv_p, sc_p, _ = _run_trunk(
        x_prompt, 0, (zeros(POOL_HIST), zeros(CONF_K - 1), zeros(SCONV_K - 1)), layers, gfin,
        mixer_tile=min(L, 512), ffn_tile=min(B * L, 512), emit_v=False)

    Bs, Ls, _ = x_sample.shape
    y_s, pool_s, conv_s, sc_s, v_s = _run_trunk(
        x_sample, PAST_LEN, (state_pool, state_conv, state_short_conv), layers, gfin,
        mixer_tile=Ls, ffn_tile=Bs * Ls, emit_v=True)
    return (y_p, y_s, pool_p, pool_s, conv_p, conv_s, sc_p, sc_s, v_s)
```

```python
import functools

import jax
import jax.numpy as jnp
from jax import lax
from jax.experimental import pallas as pl
from jax.experimental.pallas import tpu as pltpu

EPS = 1e-6
W_GROUP = 256
POOL_WINDOWS = (2, 4, 8, 16)
POOL_GC = W_GROUP // len(POOL_WINDOWS)
POOL_HIST = max(POOL_WINDOWS) - 1
CONF_K = 31
SCONV_K = 3
MLP_CHUNK = 128
MLP_HEADS = 4
MLP_HD = W_GROUP // MLP_HEADS
PAST_LEN = 1024

SUBLANES = 8
HIST_PAD = 32
SC_PAD = 8
CONV_ROWS = 64
MIXER_TILE = 1024
FFN_TILE = 512
FF_CHUNK = 1536
VMEM_LIMIT_BYTES = 56 * 1024 * 1024

_F32 = jnp.float32
_BF16 = jnp.bfloat16


def _rms_norm(x, g):
    return x * lax.rsqrt(jnp.mean(x * x, axis=-1, keepdims=True) + EPS) * g


def _dot(a, b):
    return jnp.dot(a, b, preferred_element_type=_F32)


def _mixer_kernel(x_ref, hp_ref, hc_ref, hs_ref, gmix_ref, win_ref, wpool_ref, pscale_ref,
                  wdw_ref, bdw_ref, lng_ref, lnb_ref, wsc_ref, wcat_ref, bias_ref, wout_ref,
                  *rest, tile, chunk, start_pos, emit_v):
    if emit_v:
        xo_ref, np_ref, nc_ref, ns_ref, v_ref = rest[:5]
        scratch = rest[5:]
    else:
        xo_ref, np_ref, nc_ref, ns_ref = rest[:4]
        v_ref = None
        scratch = rest[4:]
    pbuf, s2buf, s4buf, s8buf, zsh, cbuf, hbuf, pseg, mixbuf = scratch
    T, G, H = tile, W_GROUP, HIST_PAD
    l = pl.program_id(1)
    zbuf = zsh.at[0]

    @pl.when(l == 0)
    def _load_history():
        pbuf[0:H, :] = jnp.zeros((H, G), _F32)
        pbuf[H - POOL_HIST:H, :] = hp_ref[0]
        zbuf[0:H, :] = jnp.zeros((H, G), _F32)
        zbuf[H - (CONF_K - 1):H, :] = hc_ref[0]
        cbuf[0:SC_PAD, :] = jnp.zeros((SC_PAD, G), _F32)
        cbuf[SC_PAD - (SCONV_K - 1):SC_PAD, :] = hs_ref[0]

    hbuf[...] = _rms_norm(x_ref[0], gmix_ref[...]).astype(_BF16)
    pbuf[H:H + T, :] = _dot(hbuf[...], win_ref[:, 0:G])
    pseg[...] = _dot(hbuf[...], win_ref[:, G:8 * G])
    seg = lambda i, r0=0, rows=T: pseg[r0:r0 + rows, (i - 1) * G:i * G]

    zbuf[H:H + T, :] = seg(1) * jax.nn.sigmoid(seg(2))
    for r in range(1, SUBLANES):
        zsh[r, 0:H + T - SUBLANES, :] = zbuf[r:r + H + T - SUBLANES, :]
    R = min(CONV_ROWS, T)
    for r0 in range(0, T, R):
        acc = None
        for k in range(CONF_K):
            off = H - (CONF_K - 1) + k
            r = off % SUBLANES
            taps = zsh[r, off - r + r0:off - r + r0 + R, :].reshape(R // SUBLANES, SUBLANES, G)
            term = wdw_ref[k][None] * taps
            acc = term if acc is None else acc + term
        c = acc.reshape(R, G) + bdw_ref[...]
        mu = jnp.mean(c, axis=-1, keepdims=True)
        d = c - mu
        var = jnp.mean(d * d, axis=-1, keepdims=True)
        n = d * lax.rsqrt(var + EPS) * lng_ref[...] + lnb_ref[...]
        mixbuf[r0:r0 + R, 3 * G:4 * G] = (n * jax.nn.sigmoid(n)).astype(_BF16)
    nc_ref[0] = zbuf[H + T - (CONF_K - 1):H + T, :]
    zbuf[0:H, :] = zbuf[T:T + H, :]

    s2buf[8:H + T, :] = pbuf[8:H + T, :] + pbuf[7:H + T - 1, :]
    s4buf[16:H + T, :] = s2buf[16:H + T, :] + s2buf[14:H + T - 2, :]
    s8buf[24:H + T, :] = s4buf[24:H + T, :] + s4buf[20:H + T - 4, :]
    s16 = s8buf[H:H + T, :] + s8buf[H - 8:H + T - 8, :]
    lane = lax.broadcasted_iota(jnp.int32, (T, G), 1)
    row = lax.broadcasted_iota(jnp.int32, (T, G), 0)
    wsum = jnp.where(lane < POOL_GC, s2buf[H:H + T, :],
                     jnp.where(lane < 2 * POOL_GC, s4buf[H:H + T, :],
                               jnp.where(lane < 3 * POOL_GC, s8buf[H:H + T, :], s16)))
    width = jnp.where(lane < POOL_GC, POOL_WINDOWS[0],
                      jnp.where(lane < 2 * POOL_GC, POOL_WINDOWS[1],
                                jnp.where(lane < 3 * POOL_GC, POOL_WINDOWS[2], POOL_WINDOWS[3])))
    cnt = jnp.minimum(width, start_pos + l * T + row + 1).astype(_F32)
    pooled = wsum / cnt - pbuf[H:H + T, :]
    ya = _dot(pooled.astype(_BF16), wpool_ref[...]) * pscale_ref[...]
    mixbuf[:, 0:G] = ya.astype(_BF16)
    np_ref[0] = pbuf[H + T - POOL_HIST:H + T, :]
    pbuf[0:H, :] = pbuf[T:T + H, :]

    cbuf[SC_PAD:SC_PAD + T, :] = seg(5) * seg(3)
    conv = None
    for k in range(SCONV_K):
        off = SC_PAD - (SCONV_K - 1) + k
        term = wsc_ref[k:k + 1, :] * cbuf[off:off + T, :]
        conv = term if conv is None else conv + term
    mixbuf[:, G:2 * G] = (seg(4) * conv).astype(_BF16)
    ns_ref[0] = cbuf[SC_PAD + T - (SCONV_K - 1):SC_PAD + T, :]
    cbuf[0:SC_PAD, :] = cbuf[T:T + SC_PAD, :]

    if emit_v:
        v_ref[0] = seg(7)
    Lc = chunk
    wrow = lax.broadcasted_iota(jnp.int32, (Lc, MLP_HEADS * Lc), 0)
    wcol = lax.broadcasted_iota(jnp.int32, (Lc, MLP_HEADS * Lc), 1)
    wcat = jnp.where(lax.rem(wcol, Lc) <= wrow, wcat_ref[...], 0.0).astype(_BF16)
    head = lax.broadcasted_iota(jnp.int32, (Lc, G), 1) // MLP_HD
    for c0 in range(0, T, Lc):
        vb = seg(7, c0, Lc).astype(_BF16)
        rhs = jnp.concatenate(
            [jnp.where(head == hh, vb, jnp.zeros_like(vb)) for hh in range(MLP_HEADS)], axis=0)
        mixed = _dot(wcat, rhs) + bias_ref[...]
        mixbuf[c0:c0 + Lc, 2 * G:3 * G] = (seg(6, c0, Lc) * mixed).astype(_BF16)

    xo_ref[0] = x_ref[0] + _dot(mixbuf[...], wout_ref[...])


def _ffn_kernel(x_ref, g_ref, wg_ref, wu_ref, wd_ref, gfin_ref, o_ref, *, ff_chunks, final_norm):
    x = x_ref[...]
    f = _rms_norm(x, g_ref[...]).astype(_BF16)
    y = x
    for c0, c1 in ff_chunks:
        gate = _dot(f, wg_ref[:, c0:c1])
        up = _dot(f, wu_ref[:, c0:c1])
        hid = (gate * jax.nn.sigmoid(gate) * up).astype(_BF16)
        y = y + _dot(hid, wd_ref[c0:c1, :])
    if final_norm:
        y = _rms_norm(y, gfin_ref[...])
    o_ref[...] = y


def _resident(shape):
    return pl.BlockSpec(shape, lambda *_: (0,) * len(shape), pipeline_mode=pl.Buffered(1))


def _resident_layer(stacked, layer):
    return pl.BlockSpec((None,) + stacked.shape[1:], lambda *_: (layer, 0, 0), pipeline_mode=pl.Buffered(1))


def _mixer_call(x, hist_pool, hist_conv, hist_sc, lw, *, start_pos, emit_v):
    B, L, D = x.shape
    G = W_GROUP
    tile = min(L, MIXER_TILE)
    assert L % tile == 0 and tile % SUBLANES == 0 and tile >= HIST_PAD
    chunk = min(L, MLP_CHUNK)
    assert tile % chunk == 0
    n_l = L // tile
    seq_block = lambda rows: pl.BlockSpec((1, rows, G), lambda b, l: (b, 0, 0))
    in_specs = [
        pl.BlockSpec((1, tile, D), lambda b, l: (b, l, 0)),
        seq_block(POOL_HIST), seq_block(CONF_K - 1), seq_block(SCONV_K - 1),
        _resident((1, D)), _resident_layer(lw["w_in"], lw["layer"]), _resident((G, G)), _resident((1, G)),
        _resident((CONF_K, SUBLANES, G)), _resident((1, G)), _resident((1, G)), _resident((1, G)),
        _resident((SCONV_K, G)), _resident((chunk, MLP_HEADS * chunk)), _resident((chunk, G)),
        _resident_layer(lw["w_out"], lw["layer"]),
    ]
    out_shape = [
        jax.ShapeDtypeStruct((B, L, D), _F32),
        jax.ShapeDtypeStruct((B, POOL_HIST, G), _F32),
        jax.ShapeDtypeStruct((B, CONF_K - 1, G), _F32),
        jax.ShapeDtypeStruct((B, SCONV_K - 1, G), _F32),
    ]
    out_specs = [
        pl.BlockSpec((1, tile, D), lambda b, l: (b, l, 0)),
        seq_block(POOL_HIST), seq_block(CONF_K - 1), seq_block(SCONV_K - 1),
    ]
    if emit_v:
        out_shape.append(jax.ShapeDtypeStruct((B, L, G), _F32))
        out_specs.append(pl.BlockSpec((1, tile, G), lambda b, l: (b, l, 0)))
    hist_rows = HIST_PAD + tile
    scratch_shapes = [pltpu.VMEM((hist_rows, G), _F32) for _ in range(4)] + [
        pltpu.VMEM((SUBLANES, hist_rows, G), _F32), pltpu.VMEM((SC_PAD + tile, G), _F32),
        pltpu.VMEM((tile, D), _BF16), pltpu.VMEM((tile, 7 * G), _F32), pltpu.VMEM((tile, 4 * G), _BF16)]
    wcat = jnp.transpose(lw["w_s"][:, :chunk, :chunk], (1, 0, 2)).reshape(chunk, MLP_HEADS * chunk)
    bias = jnp.repeat(lw["b_s"][:, :chunk].T, MLP_HD, axis=1)
    kernel = functools.partial(_mixer_kernel, tile=tile, chunk=chunk, start_pos=start_pos, emit_v=emit_v)
    return pl.pallas_call(
        kernel, grid=(B, n_l), in_specs=in_specs, out_specs=out_specs, out_shape=out_shape,
        scratch_shapes=scratch_shapes, name="mixer",
        compiler_params=pltpu.CompilerParams(
            dimension_semantics=("arbitrary", "arbitrary"), vmem_limit_bytes=VMEM_LIMIT_BYTES),
    )(x, hist_pool, hist_conv, hist_sc, lw["g_mix"], lw["w_in"], lw["w_pool"], lw["pool_scale"],
      lw["w_conf_dw"], lw["b_conf_dw"], lw["conf_ln_g"], lw["conf_ln_b"], lw["w_sconv"],
      wcat, bias, lw["w_out"])


def _ff_chunks(d_ff, width):
    return tuple((c0, min(c0 + width, d_ff)) for c0 in range(0, d_ff, width))


def _ffn_call(x2d, lw, g_final, *, final_norm):
    N, D = x2d.shape
    d_ff = lw["w_gate"].shape[-1]
    tile = min(N, FFN_TILE)
    assert N % tile == 0
    kernel = functools.partial(_ffn_kernel, ff_chunks=_ff_chunks(d_ff, FF_CHUNK), final_norm=final_norm)
    return pl.pallas_call(
        kernel, grid=(N // tile,),
        in_specs=[pl.BlockSpec((tile, D), lambda i: (i, 0)), _resident((1, D)),
                  _resident_layer(lw["w_gate"], lw["layer"]), _resident_layer(lw["w_up"], lw["layer"]),
                  _resident_layer(lw["w_down"], lw["layer"]), _resident((1, D))],
        out_specs=pl.BlockSpec((tile, D), lambda i: (i, 0)),
        out_shape=jax.ShapeDtypeStruct((N, D), _F32), name="ffn",
        compiler_params=pltpu.CompilerParams(
            dimension_semantics=("arbitrary",), vmem_limit_bytes=VMEM_LIMIT_BYTES),
    )(x2d, lw["g_ffn"], lw["w_gate"], lw["w_up"], lw["w_down"], g_final)


def _block_diag(w):
    g, n, m = w.shape
    out = jnp.zeros((g * n, g * m), w.dtype)
    for i in range(g):
        out = out.at[i * n:(i + 1) * n, i * m:(i + 1) * m].set(w[i])
    return out


def _sublane_rows(w):
    return jnp.broadcast_to(w[:, None, :], (w.shape[0], SUBLANES, w.shape[1]))


def _mixer_major_rows(w_out):
    G = W_GROUP
    return jnp.concatenate([w_out[:, 0:G], w_out[:, 2 * G:4 * G], w_out[:, G:2 * G]], axis=1)


def _run_trunk(x, start_pos, hists, layers, g_final, *, emit_v):
    B, L, D = x.shape
    new_pool, new_conv, new_sc, new_v = [], [], [], []
    for li, lw in enumerate(layers):
        outs = _mixer_call(x, hists[0][li], hists[1][li], hists[2][li], lw, start_pos=start_pos, emit_v=emit_v)
        x = outs[0]
        new_pool.append(outs[1])
        new_conv.append(outs[2])
        new_sc.append(outs[3])
        if emit_v:
            new_v.append(outs[4])
        x = _ffn_call(x.reshape(B * L, D), lw, g_final, final_norm=li == len(layers) - 1).reshape(B, L, D)
    stack = lambda xs: jnp.stack(xs) if xs else None
    return x, stack(new_pool), stack(new_conv), stack(new_sc), stack(new_v)


def kernel(x_prompt, x_sample, state_pool, state_conv, state_short_conv, g_mix, w_in, w_pool, pool_scale, w_conf_dw, b_conf_dw, conf_ln_g, conf_ln_b, w_sconv, w_s, b_s, w_out, g_ffn, w_gate, w_up, w_down, g_final):
    depth = w_in.shape[0]
    row = lambda v: v.reshape(1, -1)
    w_in, w_out = w_in.astype(_BF16), _mixer_major_rows(w_out).astype(_BF16)
    w_gate, w_up, w_down = w_gate.astype(_BF16), w_up.astype(_BF16), w_down.astype(_BF16)
    layers = [dict(
        layer=i, w_in=w_in, w_out=w_out, w_gate=w_gate, w_up=w_up, w_down=w_down,
        g_mix=row(g_mix[i]), w_pool=_block_diag(w_pool[i]).astype(_BF16),
        pool_scale=row(pool_scale[i]), w_conf_dw=_sublane_rows(w_conf_dw[i]), b_conf_dw=row(b_conf_dw[i]),
        conf_ln_g=row(conf_ln_g[i]), conf_ln_b=row(conf_ln_b[i]), w_sconv=w_sconv[i],
        w_s=w_s[i], b_s=b_s[i], g_ffn=row(g_ffn[i]),
    ) for i in range(depth)]
    gfin = row(g_final)

    B = x_prompt.shape[0]
    zeros = lambda rows: jnp.zeros((depth, B, rows, W_GROUP), _F32)
    y_p, pool_p, conv_p, sc_p, _ = _run_trunk(
        x_prompt, 0, (zeros(POOL_HIST), zeros(CONF_K - 1), zeros(SCONV_K - 1)), layers, gfin, emit_v=False)
    y_s, pool_s, conv_s, sc_s, v_s = _run_trunk(
        x_sample, PAST_LEN, (state_pool, state_conv, state_short_conv), layers, gfin, emit_v=True)
    return (y_p, y_s, pool_p, pool_s, conv_p, conv_s, sc_p, sc_s, v_s)
```

```python
import functools

import jax
import jax.numpy as jnp
from jax import lax
from jax.experimental import pallas as pl
from jax.experimental.pallas import tpu as pltpu

EPS = 1e-6
W_GROUP = 256
POOL_WINDOWS = (2, 4, 8, 16)
POOL_GC = W_GROUP // len(POOL_WINDOWS)
POOL_HIST = max(POOL_WINDOWS) - 1
CONF_K = 31
SCONV_K = 3
MLP_CHUNK = 128
MLP_HEADS = 4
MLP_HD = W_GROUP // MLP_HEADS
PAST_LEN = 1024

SUBLANES = 8
HIST_PAD = 32
SC_PAD = 8
CONV_ROWS = 64
MIXER_TILE = 1024
FFN_TILE = 1024
FF_CHUNK = 1024
VMEM_LIMIT_BYTES = 56 * 1024 * 1024

_F32 = jnp.float32
_BF16 = jnp.bfloat16


def _rms_norm(x, g):
    return x * lax.rsqrt(jnp.mean(x * x, axis=-1, keepdims=True) + EPS) * g


def _dot(a, b):
    return jnp.dot(a, b, preferred_element_type=_F32)


def _mixer_kernel(zero_ref, x_ref, hp_ref, hc_ref, hs_ref, gmix_ref, win_ref, wpool_ref, pscale_ref,
                  wdw_ref, bdw_ref, lng_ref, lnb_ref, wsc_ref, wcat_ref, bias_ref, wout_ref,
                  *rest, tile, chunk, start_pos, emit_v):
    if emit_v:
        xo_ref, np_ref, nc_ref, ns_ref, v_ref = rest[:5]
        scratch = rest[5:]
    else:
        xo_ref, np_ref, nc_ref, ns_ref = rest[:4]
        v_ref = None
        scratch = rest[4:]
    pbuf, s2buf, s4buf, s8buf, zsh, cbuf, hbuf, pseg, mixbuf = scratch
    T, G, H = tile, W_GROUP, HIST_PAD
    l = pl.program_id(1)
    zbuf = zsh.at[0]

    @pl.when(l == 0)
    def _load_history():
        pbuf[0:H, :] = jnp.zeros((H, G), _F32)
        pbuf[H - POOL_HIST:H, :] = hp_ref[0]
        zbuf[0:H, :] = jnp.zeros((H, G), _F32)
        zbuf[H - (CONF_K - 1):H, :] = hc_ref[0]
        cbuf[0:SC_PAD, :] = jnp.zeros((SC_PAD, G), _F32)
        cbuf[SC_PAD - (SCONV_K - 1):SC_PAD, :] = hs_ref[0]

    hbuf[...] = _rms_norm(x_ref[0], gmix_ref[...]).astype(_BF16)
    pbuf[H:H + T, :] = _dot(hbuf[...], win_ref[:, 0:G])
    pseg[...] = _dot(hbuf[...], win_ref[:, G:8 * G])
    seg = lambda i, r0=0, rows=T: pseg[r0:r0 + rows, (i - 1) * G:i * G]

    zbuf[H:H + T, :] = seg(1) * jax.nn.sigmoid(seg(2))
    for r in range(1, SUBLANES):
        zsh[r, 0:H + T - SUBLANES, :] = zbuf[r:r + H + T - SUBLANES, :]
    tail = pl.multiple_of(H + T - SUBLANES + zero_ref[0], SUBLANES)
    zsh[1, pl.ds(tail, SUBLANES), :] = zsh[0, H + T - SUBLANES:H + T, :]
    R = min(CONV_ROWS, T)

    for r0 in range(0, T, R):
        acc = None
        for k in range(CONF_K):
            off = H - (CONF_K - 1) + k
            r = off % SUBLANES
            taps = zsh[r, off - r + r0:off - r + r0 + R, :].reshape(R // SUBLANES, SUBLANES, G)
            term = wdw_ref[k][None] * taps
            acc = term if acc is None else acc + term
        c = acc.reshape(R, G) + bdw_ref[...]
        mu = jnp.mean(c, axis=-1, keepdims=True)
        d = c - mu
        var = jnp.mean(d * d, axis=-1, keepdims=True)
        n = d * lax.rsqrt(var + EPS) * lng_ref[...] + lnb_ref[...]
        mixbuf[r0:r0 + R, 3 * G:4 * G] = (n * jax.nn.sigmoid(n)).astype(_BF16)
    nc_ref[0] = zbuf[H + T - (CONF_K - 1):H + T, :]
    zbuf[0:H, :] = zbuf[T:T + H, :]

    s2buf[8:H + T, :] = pbuf[8:H + T, :] + pbuf[7:H + T - 1, :]
    s4buf[16:H + T, :] = s2buf[16:H + T, :] + s2buf[14:H + T - 2, :]
    lo, hi = slice(0, 2 * POOL_GC), slice(2 * POOL_GC, G)
    s8buf[24:H + T, hi] = s4buf[24:H + T, hi] + s4buf[20:H + T - 4, hi]
    s16 = s8buf[H:H + T, hi] + s8buf[H - 8:H + T - 8, hi]
    first = lax.broadcasted_iota(jnp.int32, (1, 2 * POOL_GC), 1) < POOL_GC
    wsum = jnp.concatenate([jnp.where(first, s2buf[H:H + T, lo], s4buf[H:H + T, lo]),
                            jnp.where(first, s8buf[H:H + T, hi], s16)], axis=1)
    per_lane = lambda w: jnp.concatenate([jnp.where(first, w[0], w[1]), jnp.where(first, w[2], w[3])], axis=1)
    width = per_lane(POOL_WINDOWS)
    E = max(POOL_WINDOWS)
    row = lax.broadcasted_iota(jnp.int32, (E, G), 0)
    cnt = jnp.minimum(width, start_pos + l * T + row + 1).astype(_F32)
    inv_width = per_lane(tuple(1.0 / w for w in POOL_WINDOWS))
    mean = jnp.concatenate([wsum[0:E] / cnt, wsum[E:T] * inv_width], axis=0)
    pooled = mean - pbuf[H:H + T, :]
    ya = _dot(pooled.astype(_BF16), wpool_ref[...]) * pscale_ref[...]
    mixbuf[:, 0:G] = ya.astype(_BF16)
    np_ref[0] = pbuf[H + T - POOL_HIST:H + T, :]
    pbuf[0:H, :] = pbuf[T:T + H, :]

    cbuf[SC_PAD:SC_PAD + T, :] = seg(5) * seg(3)
    conv = None
    for k in range(SCONV_K):
        off = SC_PAD - (SCONV_K - 1) + k
        term = wsc_ref[k:k + 1, :] * cbuf[off:off + T, :]
        conv = term if conv is None else conv + term
    mixbuf[:, G:2 * G] = (seg(4) * conv).astype(_BF16)
    ns_ref[0] = cbuf[SC_PAD + T - (SCONV_K - 1):SC_PAD + T, :]
    cbuf[0:SC_PAD, :] = cbuf[T:T + SC_PAD, :]

    if emit_v:
        v_ref[0] = seg(7)
    Lc = chunk
    wrow = lax.broadcasted_iota(jnp.int32, (Lc, MLP_HEADS * Lc), 0)
    wcol = lax.broadcasted_iota(jnp.int32, (Lc, MLP_HEADS * Lc), 1)
    wcat = jnp.where(lax.rem(wcol, Lc) <= wrow, wcat_ref[...], 0.0).astype(_BF16)
    head = lax.broadcasted_iota(jnp.int32, (Lc, G), 1) // MLP_HD
    for c0 in range(0, T, Lc):
        vb = seg(7, c0, Lc).astype(_BF16)
        rhs = jnp.concatenate(
            [jnp.where(head == hh, vb, jnp.zeros_like(vb)) for hh in range(MLP_HEADS)], axis=0)
        mixed = _dot(wcat, rhs) + bias_ref[...]
        mixbuf[c0:c0 + Lc, 2 * G:3 * G] = (seg(6, c0, Lc) * mixed).astype(_BF16)

    xo_ref[0] = x_ref[0] + _dot(mixbuf[...], wout_ref[...])


def _ffn_kernel(x_ref, g_ref, wg_ref, wu_ref, wd_ref, gfin_ref, o_ref, *, ff_chunks, final_norm):
    x = x_ref[...]
    f = _rms_norm(x, g_ref[...]).astype(_BF16)
    y = x
    for c0, c1 in ff_chunks:
        gate = _dot(f, wg_ref[:, c0:c1])
        up = _dot(f, wu_ref[:, c0:c1])
        hid = (gate * jax.nn.sigmoid(gate) * up).astype(_BF16)
        y = y + _dot(hid, wd_ref[c0:c1, :])
    if final_norm:
        y = _rms_norm(y, gfin_ref[...])
    o_ref[...] = y


def _resident(shape):
    return pl.BlockSpec(shape, lambda *_: (0,) * len(shape), pipeline_mode=pl.Buffered(1))


def _resident_layer(stacked, layer):
    return pl.BlockSpec((None,) + stacked.shape[1:], lambda *_: (layer, 0, 0), pipeline_mode=pl.Buffered(1))


def _mixer_call(x, hist_pool, hist_conv, hist_sc, lw, *, start_pos, emit_v):
    B, L, D = x.shape
    G = W_GROUP
    tile = min(L, MIXER_TILE)
    assert L % tile == 0 and tile % SUBLANES == 0 and tile >= HIST_PAD
    chunk = min(L, MLP_CHUNK)
    assert tile % chunk == 0
    n_l = L // tile
    seq_block = lambda rows: pl.BlockSpec((1, rows, G), lambda b, l: (b, 0, 0))
    in_specs = [
        pl.BlockSpec(memory_space=pltpu.SMEM),
        pl.BlockSpec((1, tile, D), lambda b, l: (b, l, 0)),
        seq_block(POOL_HIST), seq_block(CONF_K - 1), seq_block(SCONV_K - 1),
        _resident((1, D)), _resident_layer(lw["w_in"], lw["layer"]), _resident((G, G)), _resident((1, G)),
        _resident((CONF_K, SUBLANES, G)), _resident((1, G)), _resident((1, G)), _resident((1, G)),
        _resident((SCONV_K, G)), _resident((chunk, MLP_HEADS * chunk)), _resident((chunk, G)),
        _resident_layer(lw["w_out"], lw["layer"]),
    ]
    out_shape = [
        jax.ShapeDtypeStruct((B, L, D), _F32),
        jax.ShapeDtypeStruct((B, POOL_HIST, G), _F32),
        jax.ShapeDtypeStruct((B, CONF_K - 1, G), _F32),
        jax.ShapeDtypeStruct((B, SCONV_K - 1, G), _F32),
    ]
    out_specs = [
        pl.BlockSpec((1, tile, D), lambda b, l: (b, l, 0)),
        seq_block(POOL_HIST), seq_block(CONF_K - 1), seq_block(SCONV_K - 1),
    ]
    if emit_v:
        out_shape.append(jax.ShapeDtypeStruct((B, L, G), _F32))
        out_specs.append(pl.BlockSpec((1, tile, G), lambda b, l: (b, l, 0)))
    hist_rows = HIST_PAD + tile
    scratch_shapes = [pltpu.VMEM((hist_rows, G), _F32) for _ in range(4)] + [
        pltpu.VMEM((SUBLANES, hist_rows, G), _F32), pltpu.VMEM((SC_PAD + tile, G), _F32),
        pltpu.VMEM((tile, D), _BF16), pltpu.VMEM((tile, 7 * G), _F32), pltpu.VMEM((tile, 4 * G), _BF16)]
    wcat = jnp.transpose(lw["w_s"][:, :chunk, :chunk], (1, 0, 2)).reshape(chunk, MLP_HEADS * chunk)
    bias = jnp.repeat(lw["b_s"][:, :chunk].T, MLP_HD, axis=1)
    kernel = functools.partial(_mixer_kernel, tile=tile, chunk=chunk, start_pos=start_pos, emit_v=emit_v)
    return pl.pallas_call(
        kernel, grid=(B, n_l), in_specs=in_specs, out_specs=out_specs, out_shape=out_shape,
        scratch_shapes=scratch_shapes, name="mixer",
        compiler_params=pltpu.CompilerParams(
            dimension_semantics=("arbitrary", "arbitrary"), vmem_limit_bytes=VMEM_LIMIT_BYTES),
    )(jnp.zeros((1,), jnp.int32), x, hist_pool, hist_conv, hist_sc, lw["g_mix"], lw["w_in"], lw["w_pool"], lw["pool_scale"],
      lw["w_conf_dw"], lw["b_conf_dw"], lw["conf_ln_g"], lw["conf_ln_b"], lw["w_sconv"],
      wcat, bias, lw["w_out"])


def _ff_chunks(d_ff, width):
    return tuple((c0, min(c0 + width, d_ff)) for c0 in range(0, d_ff, width))


def _ffn_call(x2d, lw, g_final, *, final_norm):
    N, D = x2d.shape
    d_ff = lw["w_gate"].shape[-1]
    tile = min(N, FFN_TILE)
    assert N % tile == 0
    kernel = functools.partial(_ffn_kernel, ff_chunks=_ff_chunks(d_ff, FF_CHUNK), final_norm=final_norm)
    return pl.pallas_call(
        kernel, grid=(N // tile,),
        in_specs=[pl.BlockSpec((tile, D), lambda i: (i, 0)), _resident((1, D)),
                  _resident_layer(lw["w_gate"], lw["layer"]), _resident_layer(lw["w_up"], lw["layer"]),
                  _resident_layer(lw["w_down"], lw["layer"]), _resident((1, D))],
        out_specs=pl.BlockSpec((tile, D), lambda i: (i, 0)),
        out_shape=jax.ShapeDtypeStruct((N, D), _F32), name="ffn",
        compiler_params=pltpu.CompilerParams(
            dimension_semantics=("arbitrary",), vmem_limit_bytes=VMEM_LIMIT_BYTES),
    )(x2d, lw["g_ffn"], lw["w_gate"], lw["w_up"], lw["w_down"], g_final)


def _block_diag(w):
    g, n, m = w.shape
    out = jnp.zeros((g * n, g * m), w.dtype)
    for i in range(g):
        out = out.at[i * n:(i + 1) * n, i * m:(i + 1) * m].set(w[i])
    return out


def _sublane_rows(w):
    return jnp.broadcast_to(w[:, None, :], (w.shape[0], SUBLANES, w.shape[1]))


def _mixer_major_rows(w_out):
    G = W_GROUP
    return jnp.concatenate([w_out[:, 0:G], w_out[:, 2 * G:4 * G], w_out[:, G:2 * G]], axis=1)


def _run_trunk(x, start_pos, hists, layers, g_final, *, emit_v):
    B, L, D = x.shape
    new_pool, new_conv, new_sc, new_v = [], [], [], []
    for li, lw in enumerate(layers):
        outs = _mixer_call(x, hists[0][li], hists[1][li], hists[2][li], lw, start_pos=start_pos, emit_v=emit_v)
        x = outs[0]
        new_pool.append(outs[1])
        new_conv.append(outs[2])
        new_sc.append(outs[3])
        if emit_v:
            new_v.append(outs[4])
        x = _ffn_call(x.reshape(B * L, D), lw, g_final, final_norm=li == len(layers) - 1).reshape(B, L, D)
    stack = lambda xs: jnp.stack(xs) if xs else None
    return x, stack(new_pool), stack(new_conv), stack(new_sc), stack(new_v)


def kernel(x_prompt, x_sample, state_pool, state_conv, state_short_conv, g_mix, w_in, w_pool, pool_scale, w_conf_dw, b_conf_dw, conf_ln_g, conf_ln_b, w_sconv, w_s, b_s, w_out, g_ffn, w_gate, w_up, w_down, g_final):
    depth = w_in.shape[0]
    row = lambda v: v.reshape(1, -1)
    w_in, w_out = w_in.astype(_BF16), _mixer_major_rows(w_out).astype(_BF16)
    w_gate, w_up, w_down = w_gate.astype(_BF16), w_up.astype(_BF16), w_down.astype(_BF16)
    layers = [dict(
        layer=i, w_in=w_in, w_out=w_out, w_gate=w_gate, w_up=w_up, w_down=w_down,
        g_mix=row(g_mix[i]), w_pool=_block_diag(w_pool[i]).astype(_BF16),
        pool_scale=row(pool_scale[i]), w_conf_dw=_sublane_rows(w_conf_dw[i]), b_conf_dw=row(b_conf_dw[i]),
        conf_ln_g=row(conf_ln_g[i]), conf_ln_b=row(conf_ln_b[i]), w_sconv=w_sconv[i],
        w_s=w_s[i], b_s=b_s[i], g_ffn=row(g_ffn[i]),
    ) for i in range(depth)]
    gfin = row(g_final)

    B = x_prompt.shape[0]
    zeros = lambda rows: jnp.zeros((depth, B, rows, W_GROUP), _F32)
    y_p, pool_p, conv_p, sc_p, _ = _run_trunk(
        x_prompt, 0, (zeros(POOL_HIST), zeros(CONF_K - 1), zeros(SCONV_K - 1)), layers, gfin, emit_v=False)
    y_s, pool_s, conv_s, sc_s, v_s = _run_trunk(
        x_sample, PAST_LEN, (state_pool, state_conv, state_short_conv), layers, gfin, emit_v=True)
    return (y_p, y_s, pool_p, pool_s, conv_p, conv_s, sc_p, sc_s, v_s)
```

```python
import functools

import jax
import jax.numpy as jnp
from jax import lax
from jax.experimental import pallas as pl
from jax.experimental.pallas import tpu as pltpu

EPS = 1e-6
W_GROUP = 256
POOL_WINDOWS = (2, 4, 8, 16)
POOL_GC = W_GROUP // len(POOL_WINDOWS)
POOL_HIST = max(POOL_WINDOWS) - 1
CONF_K = 31
SCONV_K = 3
MLP_CHUNK = 128
MLP_HEADS = 4
MLP_HD = W_GROUP // MLP_HEADS
PAST_LEN = 1024

SUBLANES = 8
HIST_PAD = 32
SC_PAD = 8
CONV_ROWS = 64
MIXER_TILE = 1024
FFN_TILE = 1024
FF_CHUNK = 1536
VMEM_LIMIT_BYTES = 56 * 1024 * 1024

_F32 = jnp.float32
_BF16 = jnp.bfloat16


def _rms_norm(x, g):
    return x * lax.rsqrt(jnp.mean(x * x, axis=-1, keepdims=True) + EPS) * g


def _dot(a, b):
    return jnp.dot(a, b, preferred_element_type=_F32)


def _mixer_kernel(x_ref, hp_ref, hc_ref, hs_ref, gmix_ref, win_ref, wpool_ref, pscale_ref,
                  wdw_ref, bdw_ref, lng_ref, lnb_ref, wsc_ref, wcat_ref, bias_ref, wout_ref,
                  *rest, tile, chunk, start_pos, emit_v):
    if emit_v:
        xo_ref, np_ref, nc_ref, ns_ref, v_ref = rest[:5]
        scratch = rest[5:]
    else:
        xo_ref, np_ref, nc_ref, ns_ref = rest[:4]
        v_ref = None
        scratch = rest[4:]
    pbuf, s2buf, s4buf, s8buf, zsh, cbuf, pseg, mixbuf = scratch
    T, G, H = tile, W_GROUP, HIST_PAD
    l = pl.program_id(1)
    zbuf = zsh.at[0]

    @pl.when(l == 0)
    def _load_history():
        pbuf[0:H, :] = jnp.zeros((H, G), _F32)
        pbuf[H - POOL_HIST:H, :] = hp_ref[0]
        zbuf[0:H, :] = jnp.zeros((H, G), _F32)
        zbuf[H - (CONF_K - 1):H, :] = hc_ref[0]
        cbuf[0:SC_PAD, :] = jnp.zeros((SC_PAD, G), _F32)
        cbuf[SC_PAD - (SCONV_K - 1):SC_PAD, :] = hs_ref[0]

    pseg[...] = _dot(_rms_norm(x_ref[0], gmix_ref[...]).astype(_BF16), win_ref[...])
    seg = lambda i, r0=0, rows=T: pseg[r0:r0 + rows, i * G:(i + 1) * G]

    zbuf[H:H + T, :] = seg(1) * jax.nn.sigmoid(seg(2))
    for r in range(1, SUBLANES):
        zsh[r, 0:H + T - SUBLANES, :] = zbuf[r:r + H + T - SUBLANES, :]
    R = min(CONV_ROWS, T)
    for r0 in range(0, T, R):
        acc = None
        for k in range(CONF_K):
            off = H - (CONF_K - 1) + k
            r = off % SUBLANES
            taps = zsh[r, off - r + r0:off - r + r0 + R, :].reshape(R // SUBLANES, SUBLANES, G)
            term = wdw_ref[k][None] * taps
            acc = term if acc is None else acc + term
        c = acc.reshape(R, G) + bdw_ref[...]
        mu = jnp.mean(c, axis=-1, keepdims=True)
        d = c - mu
        var = jnp.mean(d * d, axis=-1, keepdims=True)
        n = d * lax.rsqrt(var + EPS) * lng_ref[...] + lnb_ref[...]
        mixbuf[r0:r0 + R, 3 * G:4 * G] = (n * jax.nn.sigmoid(n)).astype(_BF16)
    nc_ref[0] = zbuf[H + T - (CONF_K - 1):H + T, :]
    zbuf[0:H, :] = zbuf[T:T + H, :]

    pbuf[H:H + T, :] = seg(0)
    s2buf[8:H + T, :] = pbuf[8:H + T, :] + pbuf[7:H + T - 1, :]
    s4buf[16:H + T, :] = s2buf[16:H + T, :] + s2buf[14:H + T - 2, :]
    lo, hi = slice(0, 2 * POOL_GC), slice(2 * POOL_GC, G)
    s8buf[24:H + T, hi] = s4buf[24:H + T, hi] + s4buf[20:H + T - 4, hi]
    s16 = s8buf[H:H + T, hi] + s8buf[H - 8:H + T - 8, hi]
    first = lax.broadcasted_iota(jnp.int32, (1, 2 * POOL_GC), 1) < POOL_GC
    wsum = jnp.concatenate([jnp.where(first, s2buf[H:H + T, lo], s4buf[H:H + T, lo]),
                            jnp.where(first, s8buf[H:H + T, hi], s16)], axis=1)
    per_lane = lambda w: jnp.concatenate([jnp.where(first, w[0], w[1]), jnp.where(first, w[2], w[3])], axis=1)
    width = per_lane(POOL_WINDOWS)
    E = max(POOL_WINDOWS)
    row = lax.broadcasted_iota(jnp.int32, (E, G), 0)
    cnt = jnp.minimum(width, start_pos + l * T + row + 1).astype(_F32)
    inv_width = per_lane(tuple(1.0 / w for w in POOL_WINDOWS))
    mean = jnp.concatenate([wsum[0:E] / cnt, wsum[E:T] * inv_width], axis=0)
    pooled = mean - pbuf[H:H + T, :]
    ya = _dot(pooled.astype(_BF16), wpool_ref[...]) * pscale_ref[...]
    mixbuf[:, 0:G] = ya.astype(_BF16)
    np_ref[0] = pbuf[H + T - POOL_HIST:H + T, :]
    pbuf[0:H, :] = pbuf[T:T + H, :]

    cbuf[SC_PAD:SC_PAD + T, :] = seg(5) * seg(3)
    conv = None
    for k in range(SCONV_K):
        off = SC_PAD - (SCONV_K - 1) + k
        term = wsc_ref[k:k + 1, :] * cbuf[off:off + T, :]
        conv = term if conv is None else conv + term
    mixbuf[:, G:2 * G] = (seg(4) * conv).astype(_BF16)
    ns_ref[0] = cbuf[SC_PAD + T - (SCONV_K - 1):SC_PAD + T, :]
    cbuf[0:SC_PAD, :] = cbuf[T:T + SC_PAD, :]

    if emit_v:
        v_ref[0] = seg(7)
    Lc = chunk
    wrow = lax.broadcasted_iota(jnp.int32, (Lc, MLP_HEADS * Lc), 0)
    wcol = lax.broadcasted_iota(jnp.int32, (Lc, MLP_HEADS * Lc), 1)
    wcat = jnp.where(lax.rem(wcol, Lc) <= wrow, wcat_ref[...], 0.0).astype(_BF16)
    head = lax.broadcasted_iota(jnp.int32, (Lc, G), 1) // MLP_HD
    for c0 in range(0, T, Lc):
        vb = seg(7, c0, Lc).astype(_BF16)
        rhs = jnp.concatenate(
            [jnp.where(head == hh, vb, jnp.zeros_like(vb)) for hh in range(MLP_HEADS)], axis=0)
        mixed = _dot(wcat, rhs) + bias_ref[...]
        mixbuf[c0:c0 + Lc, 2 * G:3 * G] = (seg(6, c0, Lc) * mixed).astype(_BF16)

    xo_ref[0] = x_ref[0] + _dot(mixbuf[...], wout_ref[...])


def _ffn_kernel(x_ref, g_ref, wg_ref, wu_ref, wd_ref, gfin_ref, o_ref, *, ff_chunks, final_norm):
    x = x_ref[...]
    f = _rms_norm(x, g_ref[...]).astype(_BF16)
    y = x
    for c0, c1 in ff_chunks:
        gate = _dot(f, wg_ref[:, c0:c1])
        up = _dot(f, wu_ref[:, c0:c1])
        hid = (gate * jax.nn.sigmoid(gate) * up).astype(_BF16)
        y = y + _dot(hid, wd_ref[c0:c1, :])
    if final_norm:
        y = _rms_norm(y, gfin_ref[...])
    o_ref[...] = y


def _resident(shape):
    return pl.BlockSpec(shape, lambda *_: (0,) * len(shape), pipeline_mode=pl.Buffered(1))


def _resident_layer(stacked, layer):
    return pl.BlockSpec((None,) + stacked.shape[1:], lambda *_: (layer, 0, 0), pipeline_mode=pl.Buffered(1))


def _mixer_call(x, hist_pool, hist_conv, hist_sc, lw, *, start_pos, emit_v):
    B, L, D = x.shape
    G = W_GROUP
    tile = min(L, MIXER_TILE)
    assert L % tile == 0 and tile % SUBLANES == 0 and tile >= HIST_PAD
    chunk = min(L, MLP_CHUNK)
    assert tile % chunk == 0
    n_l = L // tile
    seq_block = lambda rows: pl.BlockSpec((1, rows, G), lambda b, l: (b, 0, 0))
    in_specs = [
        pl.BlockSpec((1, tile, D), lambda b, l: (b, l, 0)),
        seq_block(POOL_HIST), seq_block(CONF_K - 1), seq_block(SCONV_K - 1),
        _resident((1, D)), _resident_layer(lw["w_in"], lw["layer"]), _resident((G, G)), _resident((1, G)),
        _resident((CONF_K, SUBLANES, G)), _resident((1, G)), _resident((1, G)), _resident((1, G)),
        _resident((SCONV_K, G)), _resident((chunk, MLP_HEADS * chunk)), _resident((chunk, G)),
        _resident_layer(lw["w_out"], lw["layer"]),
    ]
    out_shape = [
        jax.ShapeDtypeStruct((B, L, D), _F32),
        jax.ShapeDtypeStruct((B, POOL_HIST, G), _F32),
        jax.ShapeDtypeStruct((B, CONF_K - 1, G), _F32),
        jax.ShapeDtypeStruct((B, SCONV_K - 1, G), _F32),
    ]
    out_specs = [
        pl.BlockSpec((1, tile, D), lambda b, l: (b, l, 0)),
        seq_block(POOL_HIST), seq_block(CONF_K - 1), seq_block(SCONV_K - 1),
    ]
    if emit_v:
        out_shape.append(jax.ShapeDtypeStruct((B, L, G), _F32))
        out_specs.append(pl.BlockSpec((1, tile, G), lambda b, l: (b, l, 0)))
    hist_rows = HIST_PAD + tile
    scratch_shapes = [pltpu.VMEM((hist_rows, G), _F32) for _ in range(4)] + [
        pltpu.VMEM((SUBLANES, hist_rows, G), _F32), pltpu.VMEM((SC_PAD + tile, G), _F32),
        pltpu.VMEM((tile, 8 * G), _F32), pltpu.VMEM((tile, 4 * G), _BF16)]
    wcat = jnp.transpose(lw["w_s"][:, :chunk, :chunk], (1, 0, 2)).reshape(chunk, MLP_HEADS * chunk)
    bias = jnp.repeat(lw["b_s"][:, :chunk].T, MLP_HD, axis=1)
    kernel = functools.partial(_mixer_kernel, tile=tile, chunk=chunk, start_pos=start_pos, emit_v=emit_v)
    return pl.pallas_call(
        kernel, grid=(B, n_l), in_specs=in_specs, out_specs=out_specs, out_shape=out_shape,
        scratch_shapes=scratch_shapes, name="mixer",
        compiler_params=pltpu.CompilerParams(
            dimension_semantics=("arbitrary", "arbitrary"), vmem_limit_bytes=VMEM_LIMIT_BYTES),
    )(x, hist_pool, hist_conv, hist_sc, lw["g_mix"], lw["w_in"], lw["w_pool"], lw["pool_scale"],
      lw["w_conf_dw"], lw["b_conf_dw"], lw["conf_ln_g"], lw["conf_ln_b"], lw["w_sconv"],
      wcat, bias, lw["w_out"])


def _ff_chunks(d_ff, width):
    return tuple((c0, min(c0 + width, d_ff)) for c0 in range(0, d_ff, width))


def _ffn_call(x2d, lw, g_final, *, final_norm):
    N, D = x2d.shape
    d_ff = lw["w_gate"].shape[-1]
    tile = min(N, FFN_TILE)
    assert N % tile == 0
    kernel = functools.partial(_ffn_kernel, ff_chunks=_ff_chunks(d_ff, FF_CHUNK), final_norm=final_norm)
    return pl.pallas_call(
        kernel, grid=(N // tile,),
        in_specs=[pl.BlockSpec((tile, D), lambda i: (i, 0)), _resident((1, D)),
                  _resident_layer(lw["w_gate"], lw["layer"]), _resident_layer(lw["w_up"], lw["layer"]),
                  _resident_layer(lw["w_down"], lw["layer"]), _resident((1, D))],
        out_specs=pl.BlockSpec((tile, D), lambda i: (i, 0)),
        out_shape=jax.ShapeDtypeStruct((N, D), _F32), name="ffn",
        compiler_params=pltpu.CompilerParams(
            dimension_semantics=("arbitrary",), vmem_limit_bytes=VMEM_LIMIT_BYTES),
    )(x2d, lw["g_ffn"], lw["w_gate"], lw["w_up"], lw["w_down"], g_final)


def _block_diag(w):
    g, n, m = w.shape
    out = jnp.zeros((g * n, g * m), w.dtype)
    for i in range(g):
        out = out.at[i * n:(i + 1) * n, i * m:(i + 1) * m].set(w[i])
    return out


def _sublane_rows(w):
    return jnp.broadcast_to(w[:, None, :], (w.shape[0], SUBLANES, w.shape[1]))


def _mixer_major_rows(w_out):
    G = W_GROUP
    return jnp.concatenate([w_out[:, 0:G], w_out[:, 2 * G:4 * G], w_out[:, G:2 * G]], axis=1)


def _run_trunk(x, start_pos, hists, layers, g_final, *, emit_v):
    B, L, D = x.shape
    new_pool, new_conv, new_sc, new_v = [], [], [], []
    for li, lw in enumerate(layers):
        outs = _mixer_call(x, hists[0][li], hists[1][li], hists[2][li], lw, start_pos=start_pos, emit_v=emit_v)
        x = outs[0]
        new_pool.append(outs[1])
        new_conv.append(outs[2])
        new_sc.append(outs[3])
        if emit_v:
            new_v.append(outs[4])
        x = _ffn_call(x.reshape(B * L, D), lw, g_final, final_norm=li == len(layers) - 1).reshape(B, L, D)
    stack = lambda xs: jnp.stack(xs) if xs else None
    return x, stack(new_pool), stack(new_conv), stack(new_sc), stack(new_v)


def kernel(x_prompt, x_sample, state_pool, state_conv, state_short_conv, g_mix, w_in, w_pool, pool_scale, w_conf_dw, b_conf_dw, conf_ln_g, conf_ln_b, w_sconv, w_s, b_s, w_out, g_ffn, w_gate, w_up, w_down, g_final):
    depth = w_in.shape[0]
    row = lambda v: v.reshape(1, -1)
    w_in, w_out = w_in.astype(_BF16), _mixer_major_rows(w_out).astype(_BF16)
    w_gate, w_up, w_down = w_gate.astype(_BF16), w_up.astype(_BF16), w_down.astype(_BF16)
    layers = [dict(
        layer=i, w_in=w_in, w_out=w_out, w_gate=w_gate, w_up=w_up, w_down=w_down,
        g_mix=row(g_mix[i]), w_pool=_block_diag(w_pool[i]).astype(_BF16),
        pool_scale=row(pool_scale[i]), w_conf_dw=_sublane_rows(w_conf_dw[i]), b_conf_dw=row(b_conf_dw[i]),
        conf_ln_g=row(conf_ln_g[i]), conf_ln_b=row(conf_ln_b[i]), w_sconv=w_sconv[i],
        w_s=w_s[i], b_s=b_s[i], g_ffn=row(g_ffn[i]),
    ) for i in range(depth)]
    gfin = row(g_final)

    B = x_prompt.shape[0]
    zeros = lambda rows: jnp.zeros((depth, B, rows, W_GROUP), _F32)
    y_p, pool_p, conv_p, sc_p, _ = _run_trunk(
        x_prompt, 0, (zeros(POOL_HIST), zeros(CONF_K - 1), zeros(SCONV_K - 1)), layers, gfin, emit_v=False)
    y_s, pool_s, conv_s, sc_s, v_s = _run_trunk(
        x_sample, PAST_LEN, (state_pool, state_conv, state_short_conv), layers, gfin, emit_v=True)
    return (y_p, y_s, pool_p, pool_s, conv_p, conv_s, sc_p, sc_s, v_s)
```

```python
import functools

import jax
import jax.numpy as jnp
from jax import lax
from jax.experimental import pallas as pl
from jax.experimental.pallas import tpu as pltpu

EPS = 1e-6
W_GROUP = 256
POOL_WINDOWS = (2, 4, 8, 16)
POOL_GC = W_GROUP // len(POOL_WINDOWS)
POOL_HIST = max(POOL_WINDOWS) - 1
CONF_K = 31
CONF_TAPS_PAD = 32
SCONV_K = 3
MLP_CHUNK = 128
MLP_HEADS = 4
MLP_HD = W_GROUP // MLP_HEADS
PAST_LEN = 1024

SUBLANES = 8
HIST_PAD = 32
SC_PAD = 16
CONV_ROWS = 64
MIXER_TILE = 1024
FFN_TILE = 1024
FF_CHUNK = 1536
VMEM_LIMIT_BYTES = 56 * 1024 * 1024

_F32 = jnp.float32
_BF16 = jnp.bfloat16


def _rms_norm(x, g):
    return x * lax.rsqrt(jnp.mean(x * x, axis=-1, keepdims=True) + EPS) * g


def _dot(a, b):
    return jnp.dot(a, b, preferred_element_type=_F32)


def _mixer_kernel(x_ref, win_ref, wpool_ref, wcat_ref, bias_ref, wout_ref, wdw_ref, hp_ref, hc_ref, hs_ref,
                  gmix_ref, pscale_ref, bdw_ref, lng_ref, lnb_ref, wsc_ref,
                  *rest, tile, chunk, start_pos, emit_v):
    if emit_v:
        xo_ref, np_ref, nc_ref, ns_ref, v_ref = rest[:5]
        scratch = rest[5:]
    else:
        xo_ref, np_ref, nc_ref, ns_ref = rest[:4]
        v_ref = None
        scratch = rest[4:]
    pbuf, s2buf, s4buf, s8buf, zsh, cbuf, pseg, mixbuf = scratch
    T, G, H = tile, W_GROUP, HIST_PAD
    l = pl.program_id(1)
    zbuf = zsh.at[0]

    @pl.when(l == 0)
    def _load_history():
        pbuf[0:H, :] = jnp.zeros((H, G), _F32)
        pbuf[H - POOL_HIST:H, :] = hp_ref[0]
        zbuf[0:H, :] = jnp.zeros((H, G), _F32)
        zbuf[H - (CONF_K - 1):H, :] = hc_ref[0]
        cbuf[0:SC_PAD, :] = jnp.zeros((SC_PAD, G), _F32)
        cbuf[SC_PAD - (SCONV_K - 1):SC_PAD, :] = hs_ref[0]

    pseg[...] = _dot(_rms_norm(x_ref[0], gmix_ref[...]).astype(_BF16), win_ref[...])
    seg = lambda i, r0=0, rows=T: pseg[r0:r0 + rows, i * G:(i + 1) * G]

    zbuf[H:H + T, :] = seg(1) * jax.nn.sigmoid(seg(2))
    for r in range(1, SUBLANES):
        zsh[r, 0:H + T - SUBLANES, :] = zbuf[r:r + H + T - SUBLANES, :]
    R = min(CONV_ROWS, T)
    for r0 in range(0, T, R):
        acc = None
        for k in range(CONF_K):
            off = H - (CONF_K - 1) + k
            r = off % SUBLANES
            taps = zsh[r, off - r + r0:off - r + r0 + R, :].reshape(R // SUBLANES, SUBLANES, G)
            term = wdw_ref[k][None] * taps
            acc = term if acc is None else acc + term
        c = acc.reshape(R, G) + bdw_ref[...]
        mu = jnp.mean(c, axis=-1, keepdims=True)
        d = c - mu
        var = jnp.mean(d * d, axis=-1, keepdims=True)
        n = d * lax.rsqrt(var + EPS) * lng_ref[...] + lnb_ref[...]
        mixbuf[r0:r0 + R, 3 * G:4 * G] = (n * jax.nn.sigmoid(n)).astype(_BF16)
    nc_ref[0] = zbuf[H + T - (CONF_K - 1):H + T, :]
    zbuf[0:H, :] = zbuf[T:T + H, :]

    pbuf[H:H + T, :] = seg(0)
    s2buf[8:H + T, :] = pbuf[8:H + T, :] + pbuf[7:H + T - 1, :]
    s4buf[16:H + T, :] = s2buf[16:H + T, :] + s2buf[14:H + T - 2, :]
    lo, hi = slice(0, 2 * POOL_GC), slice(2 * POOL_GC, G)
    s8buf[24:H + T, hi] = s4buf[24:H + T, hi] + s4buf[20:H + T - 4, hi]
    s16 = s8buf[H:H + T, hi] + s8buf[H - 8:H + T - 8, hi]
    first = lax.broadcasted_iota(jnp.int32, (1, 2 * POOL_GC), 1) < POOL_GC
    wsum = jnp.concatenate([jnp.where(first, s2buf[H:H + T, lo], s4buf[H:H + T, lo]),
                            jnp.where(first, s8buf[H:H + T, hi], s16)], axis=1)
    per_lane = lambda w: jnp.concatenate([jnp.where(first, w[0], w[1]), jnp.where(first, w[2], w[3])], axis=1)
    width = per_lane(POOL_WINDOWS)
    E = max(POOL_WINDOWS)
    row = lax.broadcasted_iota(jnp.int32, (E, G), 0)
    cnt = jnp.minimum(width, start_pos + l * T + row + 1).astype(_F32)
    inv_width = per_lane(tuple(1.0 / w for w in POOL_WINDOWS))
    mean = jnp.concatenate([wsum[0:E] / cnt, wsum[E:T] * inv_width], axis=0)
    pooled = mean - pbuf[H:H + T, :]
    ya = _dot(pooled.astype(_BF16), wpool_ref[...]) * pscale_ref[...]
    mixbuf[:, 0:G] = ya.astype(_BF16)
    np_ref[0] = pbuf[H + T - POOL_HIST:H + T, :]
    pbuf[0:H, :] = pbuf[T:T + H, :]

    cbuf[SC_PAD:SC_PAD + T, :] = seg(5) * seg(3)
    conv = None
    for k in range(SCONV_K):
        off = SC_PAD - (SCONV_K - 1) + k
        term = wsc_ref[k:k + 1, :] * cbuf[off:off + T, :]
        conv = term if conv is None else conv + term
    mixbuf[:, G:2 * G] = (seg(4) * conv).astype(_BF16)
    ns_ref[0] = cbuf[SC_PAD + T - (SCONV_K - 1):SC_PAD + T, :]
    cbuf[0:SC_PAD, :] = cbuf[T:T + SC_PAD, :]

    if emit_v:
        v_ref[0] = seg(7)
    Lc = chunk
    wrow = lax.broadcasted_iota(jnp.int32, (Lc, MLP_HEADS * Lc), 0)
    wcol = lax.broadcasted_iota(jnp.int32, (Lc, MLP_HEADS * Lc), 1)
    wcat = jnp.where(lax.rem(wcol, Lc) <= wrow, wcat_ref[...], 0.0).astype(_BF16)
    head = lax.broadcasted_iota(jnp.int32, (Lc, G), 1) // MLP_HD
    for c0 in range(0, T, Lc):
        vb = seg(7, c0, Lc).astype(_BF16)
        rhs = jnp.concatenate(
            [jnp.where(head == hh, vb, jnp.zeros_like(vb)) for hh in range(MLP_HEADS)], axis=0)
        mixed = _dot(wcat, rhs) + bias_ref[...]
        mixbuf[c0:c0 + Lc, 2 * G:3 * G] = (seg(6, c0, Lc) * mixed).astype(_BF16)

    xo_ref[0] = x_ref[0] + _dot(mixbuf[...], wout_ref[...])


def _ffn_kernel(x_ref, wg_ref, wu_ref, wd_ref, gains_ref, o_ref, *, ff_chunks, final_norm):
    x = x_ref[...]
    f = _rms_norm(x, gains_ref[0:1, :]).astype(_BF16)
    y = x
    for c0, c1 in ff_chunks:
        gate = _dot(f, wg_ref[:, c0:c1])
        up = _dot(f, wu_ref[:, c0:c1])
        hid = (gate * jax.nn.sigmoid(gate) * up).astype(_BF16)
        y = y + _dot(hid, wd_ref[c0:c1, :])
    if final_norm:
        y = _rms_norm(y, gains_ref[1:2, :])
    o_ref[...] = y


def _resident(shape):
    return pl.BlockSpec(shape, lambda *_: (0,) * len(shape), pipeline_mode=pl.Buffered(1))


def _resident_layer(stacked, layer):
    return pl.BlockSpec((None,) + stacked.shape[1:], lambda *_: (layer, 0, 0), pipeline_mode=pl.Buffered(1))


def _mixer_call(x, hist_pool, hist_conv, hist_sc, lw, *, start_pos, emit_v):
    B, L, D = x.shape
    G = W_GROUP
    tile = min(L, MIXER_TILE)
    assert L % tile == 0 and tile % SUBLANES == 0 and tile >= HIST_PAD
    chunk = min(L, MLP_CHUNK)
    assert tile % chunk == 0
    n_l = L // tile
    seq_block = lambda rows: pl.BlockSpec((1, rows, G), lambda b, l: (b, 0, 0))
    in_specs = [
        pl.BlockSpec((1, tile, D), lambda b, l: (b, l, 0)),
        _resident_layer(lw["w_in"], lw["layer"]), _resident((G, G)),
        _resident((chunk, MLP_HEADS * chunk)), _resident((chunk, G)), _resident_layer(lw["w_out"], lw["layer"]),
        _resident((CONF_TAPS_PAD, SUBLANES, G)),
        seq_block(POOL_HIST), seq_block(CONF_K - 1), seq_block(SCONV_K - 1),
        _resident((1, D)), _resident((1, G)), _resident((1, G)), _resident((1, G)), _resident((1, G)),
        _resident((SCONV_K, G)),
    ]
    out_shape = [
        jax.ShapeDtypeStruct((B, L, D), _F32),
        jax.ShapeDtypeStruct((B, POOL_HIST, G), _F32),
        jax.ShapeDtypeStruct((B, CONF_K - 1, G), _F32),
        jax.ShapeDtypeStruct((B, SCONV_K - 1, G), _F32),
    ]
    out_specs = [
        pl.BlockSpec((1, tile, D), lambda b, l: (b, l, 0)),
        seq_block(POOL_HIST), seq_block(CONF_K - 1), seq_block(SCONV_K - 1),
    ]
    if emit_v:
        out_shape.append(jax.ShapeDtypeStruct((B, L, G), _F32))
        out_specs.append(pl.BlockSpec((1, tile, G), lambda b, l: (b, l, 0)))
    hist_rows = HIST_PAD + tile
    scratch_shapes = [pltpu.VMEM((hist_rows, G), _F32) for _ in range(4)] + [
        pltpu.VMEM((SUBLANES, hist_rows, G), _F32), pltpu.VMEM((SC_PAD + tile, G), _F32),
        pltpu.VMEM((tile, 8 * G), _F32), pltpu.VMEM((tile, 4 * G), _BF16)]
    wcat = jnp.transpose(lw["w_s"][:, :chunk, :chunk], (1, 0, 2)).reshape(chunk, MLP_HEADS * chunk)
    bias = jnp.repeat(lw["b_s"][:, :chunk].T, MLP_HD, axis=1)
    kernel = functools.partial(_mixer_kernel, tile=tile, chunk=chunk, start_pos=start_pos, emit_v=emit_v)
    return pl.pallas_call(
        kernel, grid=(B, n_l), in_specs=in_specs, out_specs=out_specs, out_shape=out_shape,
        scratch_shapes=scratch_shapes, name="mixer",
        compiler_params=pltpu.CompilerParams(
            dimension_semantics=("arbitrary", "arbitrary"), vmem_limit_bytes=VMEM_LIMIT_BYTES),
    )(x, lw["w_in"], lw["w_pool"], wcat, bias, lw["w_out"], lw["w_conf_dw"], hist_pool, hist_conv, hist_sc,
      lw["g_mix"], lw["pool_scale"], lw["b_conf_dw"], lw["conf_ln_g"], lw["conf_ln_b"], lw["w_sconv"])


def _ff_chunks(d_ff, width):
    return tuple((c0, min(c0 + width, d_ff)) for c0 in range(0, d_ff, width))


def _ffn_call(x2d, lw, g_final, *, final_norm):
    N, D = x2d.shape
    d_ff = lw["w_gate"].shape[-1]
    tile = min(N, FFN_TILE)
    assert N % tile == 0
    kernel = functools.partial(_ffn_kernel, ff_chunks=_ff_chunks(d_ff, FF_CHUNK), final_norm=final_norm)
    gains = jnp.concatenate([lw["g_ffn"], g_final, jnp.zeros((SUBLANES - 2, D), _F32)], axis=0)
    return pl.pallas_call(
        kernel, grid=(N // tile,),
        in_specs=[pl.BlockSpec((tile, D), lambda i: (i, 0)),
                  _resident_layer(lw["w_gate"], lw["layer"]), _resident_layer(lw["w_up"], lw["layer"]),
                  _resident_layer(lw["w_down"], lw["layer"]), _resident((SUBLANES, D))],
        out_specs=pl.BlockSpec((tile, D), lambda i: (i, 0)),
        out_shape=jax.ShapeDtypeStruct((N, D), _F32), name="ffn",
        compiler_params=pltpu.CompilerParams(
            dimension_semantics=("arbitrary",), vmem_limit_bytes=VMEM_LIMIT_BYTES),
    )(x2d, lw["w_gate"], lw["w_up"], lw["w_down"], gains)


def _block_diag(w):
    g, n, m = w.shape
    out = jnp.zeros((g * n, g * m), w.dtype)
    for i in range(g):
        out = out.at[i * n:(i + 1) * n, i * m:(i + 1) * m].set(w[i])
    return out


def _sublane_rows(w):
    taps = jnp.broadcast_to(w[:, None, :], (w.shape[0], SUBLANES, w.shape[1]))
    return jnp.pad(taps, ((0, CONF_TAPS_PAD - w.shape[0]), (0, 0), (0, 0)))


def _mixer_major_rows(w_out):
    G = W_GROUP
    return jnp.concatenate([w_out[:, 0:G], w_out[:, 2 * G:4 * G], w_out[:, G:2 * G]], axis=1)


def _run_trunk(x, start_pos, hists, layers, g_final, *, emit_v):
    B, L, D = x.shape
    new_pool, new_conv, new_sc, new_v = [], [], [], []
    for li, lw in enumerate(layers):
        outs = _mixer_call(x, hists[0][li], hists[1][li], hists[2][li], lw, start_pos=start_pos, emit_v=emit_v)
        x = outs[0]
        new_pool.append(outs[1])
        new_conv.append(outs[2])
        new_sc.append(outs[3])
        if emit_v:
            new_v.append(outs[4])
        x = _ffn_call(x.reshape(B * L, D), lw, g_final, final_norm=li == len(layers) - 1).reshape(B, L, D)
    stack = lambda xs: jnp.stack(xs) if xs else None
    return x, stack(new_pool), stack(new_conv), stack(new_sc), stack(new_v)


def kernel(x_prompt, x_sample, state_pool, state_conv, state_short_conv, g_mix, w_in, w_pool, pool_scale, w_conf_dw, b_conf_dw, conf_ln_g, conf_ln_b, w_sconv, w_s, b_s, w_out, g_ffn, w_gate, w_up, w_down, g_final):
    depth = w_in.shape[0]
    row = lambda v: v.reshape(1, -1)
    w_in, w_out = w_in.astype(_BF16), _mixer_major_rows(w_out).astype(_BF16)
    w_gate, w_up, w_down = w_gate.astype(_BF16), w_up.astype(_BF16), w_down.astype(_BF16)
    layers = [dict(
        layer=i, w_in=w_in, w_out=w_out, w_gate=w_gate, w_up=w_up, w_down=w_down,
        g_mix=row(g_mix[i]), w_pool=_block_diag(w_pool[i]).astype(_BF16),
        pool_scale=row(pool_scale[i]), w_conf_dw=_sublane_rows(w_conf_dw[i]), b_conf_dw=row(b_conf_dw[i]),
        conf_ln_g=row(conf_ln_g[i]), conf_ln_b=row(conf_ln_b[i]), w_sconv=w_sconv[i],
        w_s=w_s[i], b_s=b_s[i], g_ffn=row(g_ffn[i]),
    ) for i in range(depth)]
    gfin = row(g_final)

    B = x_prompt.shape[0]
    zeros = lambda rows: jnp.zeros((depth, B, rows, W_GROUP), _F32)
    y_p, pool_p, conv_p, sc_p, _ = _run_trunk(
        x_prompt, 0, (zeros(POOL_HIST), zeros(CONF_K - 1), zeros(SCONV_K - 1)), layers, gfin, emit_v=False)
    y_s, pool_s, conv_s, sc_s, v_s = _run_trunk(
        x_sample, PAST_LEN, (state_pool, state_conv, state_short_conv), layers, gfin, emit_v=True)
    return (y_p, y_s, pool_p, pool_s, conv_p, conv_s, sc_p, sc_s, v_s)
```

```python
import functools

import jax
import jax.numpy as jnp
from jax import lax
from jax.experimental import pallas as pl
from jax.experimental.pallas import tpu as pltpu

EPS = 1e-6
W_GROUP = 256
POOL_WINDOWS = (2, 4, 8, 16)
POOL_GC = W_GROUP // len(POOL_WINDOWS)
POOL_HIST = max(POOL_WINDOWS) - 1
CONF_K = 31
CONF_TAPS_PAD = 32
SCONV_K = 3
MLP_CHUNK = 128
MLP_HEADS = 4
MLP_HD = W_GROUP // MLP_HEADS
PAST_LEN = 1024

SUBLANES = 8
HIST_PAD = 32
SC_PAD = 16
CONV_ROWS = 32
MIXER_TILE = 1024
FFN_TILE = 1024
FF_CHUNK = 1536
VMEM_LIMIT_BYTES = 56 * 1024 * 1024

_F32 = jnp.float32
_BF16 = jnp.bfloat16


def _rms_norm(x, g):
    return x * lax.rsqrt(jnp.mean(x * x, axis=-1, keepdims=True) + EPS) * g


def _dot(a, b):
    return jnp.dot(a, b, preferred_element_type=_F32)


def _mixer_kernel(x_ref, win_ref, wpool_ref, wcat_ref, bias_ref, wout_ref, wdw_ref, hp_ref, hc_ref, hs_ref,
                  gmix_ref, pscale_ref, bdw_ref, lng_ref, lnb_ref, wsc_ref,
                  *rest, tile, chunk, start_pos, emit_v):
    if emit_v:
        xo_ref, np_ref, nc_ref, ns_ref, v_ref = rest[:5]
        scratch = rest[5:]
    else:
        xo_ref, np_ref, nc_ref, ns_ref = rest[:4]
        v_ref = None
        scratch = rest[4:]
    pbuf, s2buf, s4buf, s8buf, zsh, cbuf, pseg, mixbuf = scratch
    T, G, H = tile, W_GROUP, HIST_PAD
    l = pl.program_id(1)
    zbuf = zsh.at[0]

    @pl.when(l == 0)
    def _load_history():
        pbuf[0:H, :] = jnp.zeros((H, G), _F32)
        pbuf[H - POOL_HIST:H, :] = hp_ref[0]
        zbuf[0:H, :] = jnp.zeros((H, G), _F32)
        zbuf[H - (CONF_K - 1):H, :] = hc_ref[0]
        cbuf[0:SC_PAD, :] = jnp.zeros((SC_PAD, G), _F32)
        cbuf[SC_PAD - (SCONV_K - 1):SC_PAD, :] = hs_ref[0]

    pseg[...] = _dot(_rms_norm(x_ref[0], gmix_ref[...]).astype(_BF16), win_ref[...])
    seg = lambda i, r0=0, rows=T: pseg[r0:r0 + rows, i * G:(i + 1) * G]

    zbuf[H:H + T, :] = seg(1) * jax.nn.sigmoid(seg(2))
    for r in range(1, SUBLANES):
        zsh[r, 0:H + T - SUBLANES, :] = zbuf[r:r + H + T - SUBLANES, :]
    R = min(CONV_ROWS, T)
    for r0 in range(0, T, R):
        acc = None
        for k in range(CONF_K):
            off = H - (CONF_K - 1) + k
            r = off % SUBLANES
            taps = zsh[r, off - r + r0:off - r + r0 + R, :].reshape(R // SUBLANES, SUBLANES, G)
            term = wdw_ref[k][None] * taps
            acc = term if acc is None else acc + term
        c = acc.reshape(R, G) + bdw_ref[...]
        mu = jnp.mean(c, axis=-1, keepdims=True)
        d = c - mu
        var = jnp.mean(d * d, axis=-1, keepdims=True)
        n = d * lax.rsqrt(var + EPS) * lng_ref[...] + lnb_ref[...]
        mixbuf[r0:r0 + R, 3 * G:4 * G] = (n * jax.nn.sigmoid(n)).astype(_BF16)
    nc_ref[0] = zbuf[H + T - (CONF_K - 1):H + T, :]
    zbuf[0:H, :] = zbuf[T:T + H, :]

    pbuf[H:H + T, :] = seg(0)
    s2buf[8:H + T, :] = pbuf[8:H + T, :] + pbuf[7:H + T - 1, :]
    s4buf[16:H + T, :] = s2buf[16:H + T, :] + s2buf[14:H + T - 2, :]
    lo, hi = slice(0, 2 * POOL_GC), slice(2 * POOL_GC, G)
    s8buf[24:H + T, hi] = s4buf[24:H + T, hi] + s4buf[20:H + T - 4, hi]
    s16 = s8buf[H:H + T, hi] + s8buf[H - 8:H + T - 8, hi]
    first = lax.broadcasted_iota(jnp.int32, (1, 2 * POOL_GC), 1) < POOL_GC
    wsum = jnp.concatenate([jnp.where(first, s2buf[H:H + T, lo], s4buf[H:H + T, lo]),
                            jnp.where(first, s8buf[H:H + T, hi], s16)], axis=1)
    per_lane = lambda w: jnp.concatenate([jnp.where(first, w[0], w[1]), jnp.where(first, w[2], w[3])], axis=1)
    width = per_lane(POOL_WINDOWS)
    E = max(POOL_WINDOWS)
    row = lax.broadcasted_iota(jnp.int32, (E, G), 0)
    cnt = jnp.minimum(width, start_pos + l * T + row + 1).astype(_F32)
    inv_width = per_lane(tuple(1.0 / w for w in POOL_WINDOWS))
    mean = jnp.concatenate([wsum[0:E] / cnt, wsum[E:T] * inv_width], axis=0)
    pooled = mean - pbuf[H:H + T, :]
    ya = _dot(pooled.astype(_BF16), wpool_ref[...]) * pscale_ref[...]
    mixbuf[:, 0:G] = ya.astype(_BF16)
    np_ref[0] = pbuf[H + T - POOL_HIST:H + T, :]
    pbuf[0:H, :] = pbuf[T:T + H, :]

    cbuf[SC_PAD:SC_PAD + T, :] = seg(5) * seg(3)
    conv = None
    for k in range(SCONV_K):
        off = SC_PAD - (SCONV_K - 1) + k
        term = wsc_ref[k:k + 1, :] * cbuf[off:off + T, :]
        conv = term if conv is None else conv + term
    mixbuf[:, G:2 * G] = (seg(4) * conv).astype(_BF16)
    ns_ref[0] = cbuf[SC_PAD + T - (SCONV_K - 1):SC_PAD + T, :]
    cbuf[0:SC_PAD, :] = cbuf[T:T + SC_PAD, :]

    if emit_v:
        v_ref[0] = seg(7)
    Lc = chunk
    wrow = lax.broadcasted_iota(jnp.int32, (Lc, MLP_HEADS * Lc), 0)
    wcol = lax.broadcasted_iota(jnp.int32, (Lc, MLP_HEADS * Lc), 1)
    wcat = jnp.where(lax.rem(wcol, Lc) <= wrow, wcat_ref[...], 0.0).astype(_BF16)
    head = lax.broadcasted_iota(jnp.int32, (Lc, G), 1) // MLP_HD
    for c0 in range(0, T, Lc):
        vb = seg(7, c0, Lc).astype(_BF16)
        rhs = jnp.concatenate(
            [jnp.where(head == hh, vb, jnp.zeros_like(vb)) for hh in range(MLP_HEADS)], axis=0)
        mixed = _dot(wcat, rhs) + bias_ref[...]
        mixbuf[c0:c0 + Lc, 2 * G:3 * G] = (seg(6, c0, Lc) * mixed).astype(_BF16)

    xo_ref[0] = x_ref[0] + _dot(mixbuf[...], wout_ref[...])


def _ffn_kernel(x_ref, wg_ref, wu_ref, wd_ref, gains_ref, o_ref, *, ff_chunks, final_norm):
    x = x_ref[...]
    f = _rms_norm(x, gains_ref[0:1, :]).astype(_BF16)
    y = x
    for c0, c1 in ff_chunks:
        gate = _dot(f, wg_ref[:, c0:c1])
        up = _dot(f, wu_ref[:, c0:c1])
        hid = (gate * jax.nn.sigmoid(gate) * up).astype(_BF16)
        y = y + _dot(hid, wd_ref[c0:c1, :])
    if final_norm:
        y = _rms_norm(y, gains_ref[1:2, :])
    o_ref[...] = y


def _resident(shape):
    return pl.BlockSpec(shape, lambda *_: (0,) * len(shape), pipeline_mode=pl.Buffered(1))


def _resident_layer(stacked, layer):
    return pl.BlockSpec((None,) + stacked.shape[1:], lambda *_: (layer, 0, 0), pipeline_mode=pl.Buffered(1))


def _mixer_call(x, hist_pool, hist_conv, hist_sc, lw, *, start_pos, emit_v):
    B, L, D = x.shape
    G = W_GROUP
    tile = min(L, MIXER_TILE)
    assert L % tile == 0 and tile % SUBLANES == 0 and tile >= HIST_PAD
    chunk = min(L, MLP_CHUNK)
    assert tile % chunk == 0
    n_l = L // tile
    seq_block = lambda rows: pl.BlockSpec((1, rows, G), lambda b, l: (b, 0, 0))
    in_specs = [
        pl.BlockSpec((1, tile, D), lambda b, l: (b, l, 0)),
        _resident_layer(lw["w_in"], lw["layer"]), _resident((G, G)),
        _resident((chunk, MLP_HEADS * chunk)), _resident((chunk, G)), _resident_layer(lw["w_out"], lw["layer"]),
        _resident((CONF_TAPS_PAD, SUBLANES, G)),
        seq_block(POOL_HIST), seq_block(CONF_K - 1), seq_block(SCONV_K - 1),
        _resident((1, D)), _resident((1, G)), _resident((1, G)), _resident((1, G)), _resident((1, G)),
        _resident((SCONV_K, G)),
    ]
    out_shape = [
        jax.ShapeDtypeStruct((B, L, D), _F32),
        jax.ShapeDtypeStruct((B, POOL_HIST, G), _F32),
        jax.ShapeDtypeStruct((B, CONF_K - 1, G), _F32),
        jax.ShapeDtypeStruct((B, SCONV_K - 1, G), _F32),
    ]
    out_specs = [
        pl.BlockSpec((1, tile, D), lambda b, l: (b, l, 0)),
        seq_block(POOL_HIST), seq_block(CONF_K - 1), seq_block(SCONV_K - 1),
    ]
    if emit_v:
        out_shape.append(jax.ShapeDtypeStruct((B, L, G), _F32))
        out_specs.append(pl.BlockSpec((1, tile, G), lambda b, l: (b, l, 0)))
    hist_rows = HIST_PAD + tile
    scratch_shapes = [pltpu.VMEM((hist_rows, G), _F32) for _ in range(4)] + [
        pltpu.VMEM((SUBLANES, hist_rows, G), _F32), pltpu.VMEM((SC_PAD + tile, G), _F32),
        pltpu.VMEM((tile, 8 * G), _F32), pltpu.VMEM((tile, 4 * G), _BF16)]
    wcat = jnp.transpose(lw["w_s"][:, :chunk, :chunk], (1, 0, 2)).reshape(chunk, MLP_HEADS * chunk)
    bias = jnp.repeat(lw["b_s"][:, :chunk].T, MLP_HD, axis=1)
    kernel = functools.partial(_mixer_kernel, tile=tile, chunk=chunk, start_pos=start_pos, emit_v=emit_v)
    return pl.pallas_call(
        kernel, grid=(B, n_l), in_specs=in_specs, out_specs=out_specs, out_shape=out_shape,
        scratch_shapes=scratch_shapes, name="mixer",
        compiler_params=pltpu.CompilerParams(
            dimension_semantics=("arbitrary", "arbitrary"), vmem_limit_bytes=VMEM_LIMIT_BYTES),
    )(x, lw["w_in"], lw["w_pool"], wcat, bias, lw["w_out"], lw["w_conf_dw"], hist_pool, hist_conv, hist_sc,
      lw["g_mix"], lw["pool_scale"], lw["b_conf_dw"], lw["conf_ln_g"], lw["conf_ln_b"], lw["w_sconv"])


def _ff_chunks(d_ff, width):
    return tuple((c0, min(c0 + width, d_ff)) for c0 in range(0, d_ff, width))


def _ffn_call(x2d, lw, g_final, *, final_norm):
    N, D = x2d.shape
    d_ff = lw["w_gate"].shape[-1]
    tile = min(N, FFN_TILE)
    assert N % tile == 0
    kernel = functools.partial(_ffn_kernel, ff_chunks=_ff_chunks(d_ff, FF_CHUNK), final_norm=final_norm)
    gains = jnp.concatenate([lw["g_ffn"], g_final, jnp.zeros((SUBLANES - 2, D), _F32)], axis=0)
    return pl.pallas_call(
        kernel, grid=(N // tile,),
        in_specs=[pl.BlockSpec((tile, D), lambda i: (i, 0)),
                  _resident_layer(lw["w_gate"], lw["layer"]), _resident_layer(lw["w_up"], lw["layer"]),
                  _resident_layer(lw["w_down"], lw["layer"]), _resident((SUBLANES, D))],
        out_specs=pl.BlockSpec((tile, D), lambda i: (i, 0)),
        out_shape=jax.ShapeDtypeStruct((N, D), _F32), name="ffn",
        compiler_params=pltpu.CompilerParams(
            dimension_semantics=("arbitrary",), vmem_limit_bytes=VMEM_LIMIT_BYTES),
    )(x2d, lw["w_gate"], lw["w_up"], lw["w_down"], gains)


def _block_diag(w):
    g, n, m = w.shape
    out = jnp.zeros((g * n, g * m), w.dtype)
    for i in range(g):
        out = out.at[i * n:(i + 1) * n, i * m:(i + 1) * m].set(w[i])
    return out


def _sublane_rows(w):
    taps = jnp.broadcast_to(w[:, None, :], (w.shape[0], SUBLANES, w.shape[1]))
    return jnp.pad(taps, ((0, CONF_TAPS_PAD - w.shape[0]), (0, 0), (0, 0)))


def _mixer_major_rows(w_out):
    G = W_GROUP
    return jnp.concatenate([w_out[:, 0:G], w_out[:, 2 * G:4 * G], w_out[:, G:2 * G]], axis=1)


def _run_trunk(x, start_pos, hists, layers, g_final, *, emit_v):
    B, L, D = x.shape
    new_pool, new_conv, new_sc, new_v = [], [], [], []
    for li, lw in enumerate(layers):
        outs = _mixer_call(x, hists[0][li], hists[1][li], hists[2][li], lw, start_pos=start_pos, emit_v=emit_v)
        x = outs[0]
        new_pool.append(outs[1])
        new_conv.append(outs[2])
        new_sc.append(outs[3])
        if emit_v:
            new_v.append(outs[4])
        x = _ffn_call(x.reshape(B * L, D), lw, g_final, final_norm=li == len(layers) - 1).reshape(B, L, D)
    stack = lambda xs: jnp.stack(xs) if xs else None
    return x, stack(new_pool), stack(new_conv), stack(new_sc), stack(new_v)


def kernel(x_prompt, x_sample, state_pool, state_conv, state_short_conv, g_mix, w_in, w_pool, pool_scale, w_conf_dw, b_conf_dw, conf_ln_g, conf_ln_b, w_sconv, w_s, b_s, w_out, g_ffn, w_gate, w_up, w_down, g_final):
    depth = w_in.shape[0]
    row = lambda v: v.reshape(1, -1)
    w_in, w_out = w_in.astype(_BF16), _mixer_major_rows(w_out).astype(_BF16)
    w_gate, w_up, w_down = w_gate.astype(_BF16), w_up.astype(_BF16), w_down.astype(_BF16)
    layers = [dict(
        layer=i, w_in=w_in, w_out=w_out, w_gate=w_gate, w_up=w_up, w_down=w_down,
        g_mix=row(g_mix[i]), w_pool=_block_diag(w_pool[i]).astype(_BF16),
        pool_scale=row(pool_scale[i]), w_conf_dw=_sublane_rows(w_conf_dw[i]), b_conf_dw=row(b_conf_dw[i]),
        conf_ln_g=row(conf_ln_g[i]), conf_ln_b=row(conf_ln_b[i]), w_sconv=w_sconv[i],
        w_s=w_s[i], b_s=b_s[i], g_ffn=row(g_ffn[i]),
    ) for i in range(depth)]
    gfin = row(g_final)

    B = x_prompt.shape[0]
    zeros = lambda rows: jnp.zeros((depth, B, rows, W_GROUP), _F32)
    y_p, pool_p, conv_p, sc_p, _ = _run_trunk(
        x_prompt, 0, (zeros(POOL_HIST), zeros(CONF_K - 1), zeros(SCONV_K - 1)), layers, gfin, emit_v=False)
    y_s, pool_s, conv_s, sc_s, v_s = _run_trunk(
        x_sample, PAST_LEN, (state_pool, state_conv, state_short_conv), layers, gfin, emit_v=True)
    return (y_p, y_s, pool_p, pool_s, conv_p, conv_s, sc_p, sc_s, v_s)
```
